```python
import math
import jax, jax.numpy as jnp
from jax import lax
import numpy as np

D_MODEL = 2048
BATCH = 4
SEQ = 4096
DEPTH = 4

WIDTH_A = D_MODEL // 4
WIDTH_B = D_MODEL // 2
WIDTH_C = D_MODEL // 4
MIX_WIDTH = WIDTH_A + WIDTH_B + WIDTH_C
CHUNK = 128
GROUP_A = 128
N_GROUPS_A = WIDTH_A // GROUP_A
DIFF_HEAD_DIM = 64
V_HEAD_DIM = 2 * DIFF_HEAD_DIM
N_HEADS_B = WIDTH_B // V_HEAD_DIM
Q_BLOCK = 128
CONV_WIDTH = 3
N_GROUPS_C = WIDTH_C // 128
D_FF = ((8 * D_MODEL // 3 + 255) // 256) * 256
EPS = 1e-6

SPLIT_SIZES = [WIDTH_A, WIDTH_A,
               WIDTH_B, WIDTH_B, WIDTH_B,
               WIDTH_C, WIDTH_C, WIDTH_C]
IN_COLS = sum(SPLIT_SIZES)

kernel_name = "hybrid_gmlp_diffattn_shortconv_macaron"


def rms_norm(x, g):
    xf = x.astype(jnp.float32)
    y = xf * lax.rsqrt(jnp.mean(xf * xf, axis=-1, keepdims=True) + EPS)
    return (y * g.astype(jnp.float32)).astype(x.dtype)


def swiglu(h, w_gate, w_up, w_down):
    return (jax.nn.silu(h @ w_gate) * (h @ w_up)) @ w_down


def alibi_slopes(n_heads):
    return jnp.exp2(-8.0 * jnp.arange(1, n_heads + 1, dtype=jnp.float32) / n_heads)


def spatial_gating(u, v, g_v, w_s, b_s):
    bsz, s_len, _ = u.shape
    v = rms_norm(v, g_v)
    vc = v.reshape(bsz, s_len // CHUNK, CHUNK, N_GROUPS_A, GROUP_A)
    causal = jnp.tril(jnp.ones((CHUNK, CHUNK), dtype=bool))
    ws = jnp.where(causal[None], w_s, jnp.zeros_like(w_s))
    mixed = jnp.einsum('gts,bnsgc->bntgc', ws, vc) + b_s.T[None, None, :, :, None]
    return u * mixed.reshape(bsz, s_len, WIDTH_A)


def diff_attention(q, k, v, lam, g_sub, lambda_init):
    bsz, s_len, _ = q.shape
    nb = s_len // Q_BLOCK
    scale = DIFF_HEAD_DIM ** -0.5
    qb = q.reshape(bsz, nb, Q_BLOCK, N_HEADS_B, 2, DIFF_HEAD_DIM).transpose(1, 0, 3, 4, 2, 5)
    k = k.reshape(bsz, s_len, N_HEADS_B, 2, DIFF_HEAD_DIM)
    v = v.reshape(bsz, s_len, N_HEADS_B, V_HEAD_DIM)
    slopes = alibi_slopes(N_HEADS_B)
    kpos = jnp.arange(s_len)

    def block(args):
        q_blk, start = args
        s = jnp.einsum('bhmqd,bkhmd->bhmqk', q_blk, k).astype(jnp.float32) * scale
        dist = (start + jnp.arange(Q_BLOCK))[:, None] - kpos[None, :]
        bias = -slopes[:, None, None] * dist.astype(jnp.float32)
        s = jnp.where((dist >= 0)[None, None, None], s + bias[None, :, None], -jnp.inf)
        p = jax.nn.softmax(s, axis=-1)
        a = p[:, :, 0] - lam * p[:, :, 1]
        return jnp.einsum('bhqk,bkhe->bqhe', a.astype(v.dtype), v)

    starts = jnp.arange(nb) * Q_BLOCK
    out = lax.map(block, (qb, starts))
    out = out.transpose(1, 0, 2, 3, 4).reshape(bsz, s_len, N_HEADS_B, V_HEAD_DIM)
    out = rms_norm(out, g_sub) * (1.0 - lambda_init)
    return out.reshape(bsz, s_len, WIDTH_B)


def short_gated_conv(b_gate, c_gate, x_c, w_conv):
    s_len = x_c.shape[1]
    z = c_gate * x_c
    zp = jnp.pad(z, ((0, 0), (CONV_WIDTH - 1, 0), (0, 0)))
    y = zp[:, 0:s_len] * w_conv[0]
    for j in range(1, CONV_WIDTH):
        y = y + zp[:, j:j + s_len] * w_conv[j]
    return b_gate * y


def setup_inputs(seed: int = 0) -> dict:
    key = jax.random.key(seed)
    ks = jax.random.split(key, 20)
    f32 = jnp.float32
    n = lambda k, shape, s: jax.random.normal(k, shape, f32) * s
    gain = lambda k, shape: 1.0 + 0.01 * jax.random.normal(k, shape, f32)
    return {
        "x": jax.random.normal(ks[0], (BATCH, SEQ, D_MODEL), f32),
        "g_ffn1": gain(ks[1], (DEPTH, D_MODEL)),
        "w_ffn1_gate": n(ks[2], (DEPTH, D_MODEL, D_FF), D_MODEL ** -0.5),
        "w_ffn1_up": n(ks[3], (DEPTH, D_MODEL, D_FF), D_MODEL ** -0.5),
        "w_ffn1_down": n(ks[4], (DEPTH, D_FF, D_MODEL), D_FF ** -0.5),
        "g_mix": gain(ks[5], (DEPTH, D_MODEL)),
        "w_in": n(ks[6], (DEPTH, D_MODEL, IN_COLS), D_MODEL ** -0.5),
        "g_sga_v": gain(ks[7], (DEPTH, WIDTH_A)),
        "w_sga_s": n(ks[8], (DEPTH, N_GROUPS_A, CHUNK, CHUNK), CHUNK ** -0.5),
        "b_sga_s": 1.0 + 0.1 * jax.random.normal(ks[9], (DEPTH, N_GROUPS_A, CHUNK), f32),
        "lambda_qk": n(ks[10], (DEPTH, 4, DIFF_HEAD_DIM), 0.1),
        "g_diff_sub": gain(ks[11], (DEPTH, V_HEAD_DIM)),
        "w_conv": n(ks[12], (DEPTH, CONV_WIDTH, WIDTH_C), CONV_WIDTH ** -0.5),
        "w_out": n(ks[13], (DEPTH, MIX_WIDTH, D_MODEL), MIX_WIDTH ** -0.5),
        "g_ffn2": gain(ks[14], (DEPTH, D_MODEL)),
        "w_ffn2_gate": n(ks[15], (DEPTH, D_MODEL, D_FF), D_MODEL ** -0.5),
        "w_ffn2_up": n(ks[16], (DEPTH, D_MODEL, D_FF), D_MODEL ** -0.5),
        "w_ffn2_down": n(ks[17], (DEPTH, D_FF, D_MODEL), D_FF ** -0.5),
        "g_final": gain(ks[18], (D_MODEL,)),
    }


def reference(x, g_ffn1, w_ffn1_gate, w_ffn1_up, w_ffn1_down, g_mix, w_in, g_sga_v,
              w_sga_s, b_sga_s, lambda_qk, g_diff_sub, w_conv, w_out, g_ffn2,
              w_ffn2_gate, w_ffn2_up, w_ffn2_down, g_final):
    split_points = [int(p) for p in np.cumsum(SPLIT_SIZES)[:-1]]
    for l in range(DEPTH):
        h = rms_norm(x, g_ffn1[l])
        x = x + 0.5 * swiglu(h, w_ffn1_gate[l], w_ffn1_up[l], w_ffn1_down[l])

        h = rms_norm(x, g_mix[l])
        proj = h @ w_in[l]
        u, v, q, k, vv, b_gate, c_gate, x_c = jnp.split(proj, split_points, axis=-1)

        y_a = spatial_gating(u, v, g_sga_v[l], w_sga_s[l], b_sga_s[l])

        lambda_init = 0.8 - 0.6 * math.exp(-0.3 * l)
        lq = lambda_qk[l].astype(jnp.float32)
        lam = jnp.exp(jnp.sum(lq[0] * lq[1])) - jnp.exp(jnp.sum(lq[2] * lq[3])) + lambda_init
        y_b = diff_attention(q, k, vv, lam, g_diff_sub[l], lambda_init)

        y_c = short_gated_conv(b_gate, c_gate, x_c, w_conv[l])

        x = x + jnp.concatenate([y_a, y_b, y_c], axis=-1) @ w_out[l]

        h = rms_norm(x, g_ffn2[l])
        x = x + 0.5 * swiglu(h, w_ffn2_gate[l], w_ffn2_up[l], w_ffn2_down[l])
    return rms_norm(x, g_final)
```

```python
import functools
import math

import jax
import jax.numpy as jnp
from jax import lax
from jax.experimental import pallas as pl
from jax.experimental.pallas import tpu as pltpu

F32 = jnp.float32
BF16 = jnp.bfloat16

EPS = 1e-6
CHUNK = 128
GROUP_A = 128
DIFF_HEAD_DIM = 64
V_HEAD_DIM = 2 * DIFF_HEAD_DIM
CONV_WIDTH = 3
HALO = 8
LOG2E = 1.4426950408889634
NEG_BIG = -1e30

V7X_VMEM_LIMIT_BYTES = 60 * 2**20

FFN_ROWS = 1024
FFN_COLS = 512
FFN_X_SLICES = 8
ROW_TILE = 512


def _params(ndim):
    return pltpu.CompilerParams(dimension_semantics=("arbitrary",) * ndim,
                                vmem_limit_bytes=V7X_VMEM_LIMIT_BYTES)


def _dot(a, b):
    return jnp.dot(a, b, preferred_element_type=F32)


def _rms(xf, g):
    y = xf * lax.rsqrt(jnp.mean(xf * xf, axis=-1, keepdims=True) + EPS)
    return y * g


def _resident(block_shape, index_map):
    return pl.BlockSpec(block_shape, index_map, pipeline_mode=pl.Buffered(1))


def _norm_kernel(x_ref, g_ref, o_ref):
    o_ref[...] = _rms(x_ref[...], g_ref[...]).astype(o_ref.dtype)


def _norm(x, gains, layer):
    m, d = x.shape
    return pl.pallas_call(
        _norm_kernel,
        out_shape=jax.ShapeDtypeStruct((m, d), BF16),
        grid=(m // ROW_TILE,),
        in_specs=[pl.BlockSpec((ROW_TILE, d), lambda i: (i, 0)),
                  pl.BlockSpec((None, 1, d), lambda i: (layer, 0, 0))],
        out_specs=pl.BlockSpec((ROW_TILE, d), lambda i: (i, 0)),
        compiler_params=_params(1),
        name="rmsnorm_in",
    )(x, gains)


def _ffn_kernel(h_ref, xs_ref, wg_ref, wu_ref, wd_ref, gn_ref, xo_ref, hn_ref, *, nf, rows_xs):
    f = pl.program_id(1)

    @pl.when(f == 0)
    def _():
        xo_ref[...] = jnp.zeros_like(xo_ref)

    @pl.when(f < FFN_X_SLICES)
    def _():
        r0 = pl.multiple_of(f * rows_xs, rows_xs)
        xo_ref[pl.ds(r0, rows_xs), :] += xs_ref[...]

    h = h_ref[...]
    g = _dot(h, wg_ref[...])
    u = _dot(h, wu_ref[...])
    a = ((0.5 * g) * jax.nn.sigmoid(g) * u).astype(BF16)
    xo_ref[...] += _dot(a, wd_ref[...])

    @pl.when(f == nf - 1)
    def _():
        hn_ref[...] = _rms(xo_ref[...], gn_ref[...]).astype(hn_ref.dtype)


def _ffn(h, x, wg, wu, wd, gains, layer, gain_layer, *, rows, out_dtype):
    m, d = x.shape
    d_ff = wg.shape[-1]
    nf = d_ff // FFN_COLS
    assert nf >= FFN_X_SLICES and rows % (FFN_X_SLICES * 8) == 0
    rows_xs = rows // FFN_X_SLICES
    kern = functools.partial(_ffn_kernel, nf=nf, rows_xs=rows_xs)
    return pl.pallas_call(
        kern,
        out_shape=[jax.ShapeDtypeStruct((m, d), F32), jax.ShapeDtypeStruct((m, d), out_dtype)],
        grid=(m // rows, nf),
        in_specs=[
            pl.BlockSpec((rows, d), lambda i, f: (i, 0)),
            pl.BlockSpec((rows_xs, d),
                         lambda i, f: (i * FFN_X_SLICES + jnp.minimum(f, FFN_X_SLICES - 1), 0)),
            pl.BlockSpec((None, d, FFN_COLS), lambda i, f: (layer, 0, f)),
            pl.BlockSpec((None, d, FFN_COLS), lambda i, f: (layer, 0, f)),
            pl.BlockSpec((None, FFN_COLS, d), lambda i, f: (layer, f, 0)),
            pl.BlockSpec((None, 1, d), lambda i, f: (gain_layer, 0, 0)),
        ],
        out_specs=[pl.BlockSpec((rows, d), lambda i, f: (i, 0)),
                   pl.BlockSpec((rows, d), lambda i, f: (i, 0))],
        compiler_params=_params(2),
        name="ffn_swiglu",
    )(h, x, wg, wu, wd, gains)


def _inproj_kernel(h_ref, wuv_ref, wq_ref, wk_ref, wvt_ref, wc_ref, gv_ref, ws_ref, bs_ref, wconv_ref,
                   ya_ref, q_ref, k_ref, vt_ref, yc_ref, zs_ref, *, tiles_per_seq, q_scale):
    i = pl.program_id(0)
    rows = h_ref.shape[0]
    wa = ya_ref.shape[1]
    wcw = yc_ref.shape[1]
    h = h_ref[...]

    u = _dot(h, wuv_ref[:, 0:wa])
    v = _dot(h, wuv_ref[:, wa:2 * wa])
    vn = _rms(v, gv_ref[...]).astype(BF16)
    t_idx = lax.broadcasted_iota(jnp.int32, (CHUNK, CHUNK), 0)
    s_idx = lax.broadcasted_iota(jnp.int32, (CHUNK, CHUNK), 1)
    for g in range(wa // GROUP_A):
        cs = slice(g * GROUP_A, (g + 1) * GROUP_A)
        ws_g = jnp.where(t_idx >= s_idx, ws_ref[g], 0.0).astype(BF16)
        b_g = bs_ref[:, g:g + 1]
        for c in range(rows // CHUNK):
            rs = slice(c * CHUNK, (c + 1) * CHUNK)
            mixed = _dot(ws_g, vn[rs, cs]) + b_g
            ya_ref[rs, cs] = (u[rs, cs] * mixed).astype(ya_ref.dtype)

    q_ref[...] = (_dot(h, wq_ref[...]) * q_scale).astype(q_ref.dtype)
    k_ref[...] = _dot(h, wk_ref[...]).astype(k_ref.dtype)
    vt_ref[...] = lax.dot_general(wvt_ref[...], h, (((1,), (1,)), ((), ())),
                                  preferred_element_type=F32).astype(vt_ref.dtype)

    bg = _dot(h, wc_ref[:, 0:wcw])
    cg = _dot(h, wc_ref[:, wcw:2 * wcw])
    xc = _dot(h, wc_ref[:, 2 * wcw:3 * wcw])
    z = cg * xc

    @pl.when(i % tiles_per_seq == 0)
    def _():
        zs_ref[0:HALO, :] = jnp.zeros((HALO, wcw), F32)

    zs_ref[HALO:HALO + rows, :] = z
    w = wconv_ref[...]
    y = zs_ref[HALO - 2:HALO - 2 + rows, :] * w[0:1, :]
    y = y + zs_ref[HALO - 1:HALO - 1 + rows, :] * w[1:2, :]
    y = y + z * w[2:3, :]
    yc_ref[...] = (bg * y).astype(yc_ref.dtype)
    zs_ref[0:HALO, :] = zs_ref[rows:rows + HALO, :]


def _inproj(h, wuv, wq, wk, wvt, wc, gv, ws, bs_t, wconv, layer, *, seq):
    m, d = h.shape
    rows = ROW_TILE
    wa = wuv.shape[-1] // 2
    wb = wq.shape[-1]
    wcw = wc.shape[-1] // 3
    kern = functools.partial(_inproj_kernel, tiles_per_seq=seq // rows,
                             q_scale=DIFF_HEAD_DIM ** -0.5 * LOG2E)
    lw = lambda i: (layer, 0, 0)
    return pl.pallas_call(
        kern,
        out_shape=[jax.ShapeDtypeStruct((m, wa), BF16),
                   jax.ShapeDtypeStruct((m, wb), BF16),
                   jax.ShapeDtypeStruct((m, wb), BF16),
                   jax.ShapeDtypeStruct((m // rows, wb, rows), BF16),
                   jax.ShapeDtypeStruct((m, wcw), BF16)],
        grid=(m // rows,),
        in_specs=[
            pl.BlockSpec((rows, d), lambda i: (i, 0)),
            _resident((None, d, 2 * wa), lw),
            _resident((None, d, wb), lw),
            _resident((None, d, wb), lw),
            _resident((None, wb, d), lw),
            _resident((None, d, 3 * wcw), lw),
            pl.BlockSpec((None, 1, wa), lw),
            pl.BlockSpec((None, wa // GROUP_A, CHUNK, CHUNK), lambda i: (layer, 0, 0, 0)),
            pl.BlockSpec((None, CHUNK, wa // GROUP_A), lw),
            pl.BlockSpec((None, CONV_WIDTH, wcw), lw),
        ],
        out_specs=[pl.BlockSpec((rows, wa), lambda i: (i, 0)),
                   pl.BlockSpec((rows, wb), lambda i: (i, 0)),
                   pl.BlockSpec((rows, wb), lambda i: (i, 0)),
                   pl.BlockSpec((None, wb, rows), lambda i: (i, 0, 0)),
                   pl.BlockSpec((rows, wcw), lambda i: (i, 0))],
        scratch_shapes=[pltpu.VMEM((rows + HALO, wcw), F32)],
        compiler_params=_params(1),
        name="inproj_mix_ac",
    )(h, wuv, wq, wk, wvt, wc, gv, ws, bs_t, wconv)


def _attn_kernel(lq_ref, gs_ref, q_ref, k_ref, vt_ref, o_ref, bias_ref, biasd_ref, *, lambda_init):
    hd = pl.program_id(1)
    qi = pl.program_id(2)
    t = q_ref.shape[0]
    slope2 = jnp.exp2(-(jnp.full((1, t), hd + 1, jnp.int32).astype(F32))) * LOG2E

    @pl.when(qi == 0)
    def _():
        kk = lax.broadcasted_iota(jnp.int32, (t, t), 0)
        qq = lax.broadcasted_iota(jnp.int32, (t, t), 1)
        b = (kk - qq).astype(F32) * slope2
        bias_ref[...] = b
        biasd_ref[...] = jnp.where(kk <= qq, b, NEG_BIG)

    q = q_ref[...]
    lane = lax.broadcasted_iota(jnp.int32, q.shape, 1)
    zero = jnp.zeros_like(q)
    q_maps = (jnp.where(lane < DIFF_HEAD_DIM, q, zero), jnp.where(lane >= DIFF_HEAD_DIM, q, zero))

    def tile(j, carry, bias):
        k = k_ref[j]
        vt = vt_ref[j]
        c = slope2 * ((j - qi) * t).astype(F32)
        new = []
        for mp in range(2):
            m_old, l_old, a_old = carry[mp]
            s = lax.dot_general(k, q_maps[mp], (((1,), (1,)), ((), ())),
                                preferred_element_type=F32) + bias
            m_new = jnp.maximum(m_old, jnp.max(s, axis=0, keepdims=True) + c)
            p = jnp.exp2(s - (m_new - c))
            alpha = jnp.exp2(m_old - m_new)
            l_new = l_old * alpha + jnp.sum(p, axis=0, keepdims=True)
            a_new = a_old * alpha + _dot(vt, p.astype(BF16))
            new.append((m_new, l_new, a_new))
        return tuple(new)

    init_map = (jnp.full((1, t), NEG_BIG, F32), jnp.zeros((1, t), F32), jnp.zeros((V_HEAD_DIM, t), F32))
    carry = lax.fori_loop(0, qi, lambda j, cr: tile(j, cr, bias_ref[...]), (init_map, init_map))
    (_, l1, a1), (_, l2, a2) = tile(qi, carry, biasd_ref[...])

    lq = lq_ref[...]
    s01 = jnp.sum(lq[0:1, :] * lq[1:2, :], axis=1, keepdims=True)
    s23 = jnp.sum(lq[2:3, :] * lq[3:4, :], axis=1, keepdims=True)
    lam = jnp.exp(s01) - jnp.exp(s23) + lambda_init
    o = a1 * (1.0 / l1) - lam * (a2 * (1.0 / l2))
    y = o * lax.rsqrt(jnp.mean(o * o, axis=0, keepdims=True) + EPS)
    y = y * gs_ref[...] * (1.0 - lambda_init)
    o_ref[...] = y.T.astype(o_ref.dtype)


def _attn(q, k, vt, lq, gsub, layer, *, batch, seq, lambda_init):
    m, wb = q.shape
    t = ROW_TILE
    nq = seq // t
    heads = wb // V_HEAD_DIM
    k3 = k.reshape(m // t, t, wb)
    kern = functools.partial(_attn_kernel, lambda_init=lambda_init)
    return pl.pallas_call(
        kern,
        out_shape=jax.ShapeDtypeStruct((m, wb), BF16),
        grid=(batch, heads, nq),
        in_specs=[
            pl.BlockSpec((None, 4, DIFF_HEAD_DIM), lambda b, h, i: (layer, 0, 0)),
            pl.BlockSpec((None, V_HEAD_DIM, 1), lambda b, h, i: (layer, 0, 0)),
            pl.BlockSpec((t, V_HEAD_DIM), lambda b, h, i: (b * nq + i, h)),
            pl.BlockSpec((nq, t, V_HEAD_DIM), lambda b, h, i: (b, 0, h)),
            pl.BlockSpec((nq, V_HEAD_DIM, t), lambda b, h, i: (b, h, 0)),
        ],
        out_specs=pl.BlockSpec((t, V_HEAD_DIM), lambda b, h, i: (b * nq + i, h)),
        scratch_shapes=[pltpu.VMEM((t, t), F32), pltpu.VMEM((t, t), F32)],
        compiler_params=_params(3),
        name="diff_attn",
    )(lq, gsub, q, k3, vt)


def _outproj_kernel(ya_ref, yb_ref, yc_ref, x_ref, w_ref, gn_ref, xo_ref, hn_ref):
    wa = ya_ref.shape[1]
    wb = yb_ref.shape[1]
    acc = x_ref[...] + _dot(ya_ref[...], w_ref[0:wa, :])
    acc = acc + _dot(yb_ref[...], w_ref[wa:wa + wb, :])
    acc = acc + _dot(yc_ref[...], w_ref[wa + wb:, :])
    xo_ref[...] = acc
    hn_ref[...] = _rms(acc, gn_ref[...]).astype(hn_ref.dtype)


def _outproj(ya, yb, yc, x, w, gains, layer):
    m, d = x.shape
    rows = ROW_TILE
    row_block = lambda width: pl.BlockSpec((rows, width), lambda i: (i, 0))
    return pl.pallas_call(
        _outproj_kernel,
        out_shape=[jax.ShapeDtypeStruct((m, d), F32), jax.ShapeDtypeStruct((m, d), BF16)],
        grid=(m // rows,),
        in_specs=[row_block(ya.shape[1]), row_block(yb.shape[1]), row_block(yc.shape[1]), row_block(d),
                  _resident((None,) + w.shape[1:], lambda i: (layer, 0, 0)),
                  pl.BlockSpec((None, 1, d), lambda i: (layer, 0, 0))],
        out_specs=[row_block(d), row_block(d)],
        compiler_params=_params(1),
        name="outproj",
    )(ya, yb, yc, x, w, gains)


def kernel(x, g_ffn1, w_ffn1_gate, w_ffn1_up, w_ffn1_down, g_mix, w_in, g_sga_v, w_sga_s, b_sga_s,
           lambda_qk, g_diff_sub, w_conv, w_out, g_ffn2, w_ffn2_gate, w_ffn2_up, w_ffn2_down, g_final):
    batch, seq, d = x.shape
    depth = w_in.shape[0]
    wa = g_sga_v.shape[-1]
    wb = w_out.shape[1] - 2 * wa
    assert seq % ROW_TILE == 0 and wb % V_HEAD_DIM == 0

    cast = lambda w: w.astype(BF16)
    w1g, w1u, w1d = cast(w_ffn1_gate), cast(w_ffn1_up), cast(w_ffn1_down)
    w2g, w2u, w2d = cast(w_ffn2_gate), cast(w_ffn2_up), cast(w_ffn2_down)
    o = 2 * wa
    wuv = cast(w_in[:, :, 0:o])
    wq = cast(w_in[:, :, o:o + wb])
    wk = cast(w_in[:, :, o + wb:o + 2 * wb])
    wvt = cast(jnp.swapaxes(w_in[:, :, o + 2 * wb:o + 3 * wb], 1, 2))
    wc = cast(w_in[:, :, o + 3 * wb:])
    wo = cast(w_out)
    row = lambda g: g.reshape(g.shape[0], 1, g.shape[-1])
    g1, gm, g2, gv = row(g_ffn1), row(g_mix), row(g_ffn2), row(g_sga_v)
    gf = g_final.reshape(1, 1, d)
    bs_t = jnp.swapaxes(b_sga_s, 1, 2)
    gsub = g_diff_sub.reshape(depth, V_HEAD_DIM, 1)

    xf = x.reshape(batch * seq, d)
    h = _norm(xf, g1, 0)
    for l in range(depth):
        xf, h = _ffn(h, xf, w1g, w1u, w1d, gm, l, l, rows=FFN_ROWS, out_dtype=BF16)
        ya, q, k, vt, yc = _inproj(h, wuv, wq, wk, wvt, wc, gv, w_sga_s, bs_t, w_conv, l, seq=seq)
        lambda_init = 0.8 - 0.6 * math.exp(-0.3 * l)
        yb = _attn(q, k, vt, lambda_qk, gsub, l, batch=batch, seq=seq, lambda_init=lambda_init)
        xf, h = _outproj(ya, yb, yc, xf, wo, g2, l)
        if l + 1 < depth:
            xf, h = _ffn(h, xf, w2g, w2u, w2d, g1, l, l + 1, rows=FFN_ROWS, out_dtype=BF16)
        else:
            xf, h = _ffn(h, xf, w2g, w2u, w2d, gf, l, 0, rows=ROW_TILE, out_dtype=F32)
    return h.reshape(batch, seq, d)
```

```python
import functools
import math

import jax
import jax.numpy as jnp
from jax import lax
from jax.experimental import pallas as pl
from jax.experimental.pallas import tpu as pltpu

F32 = jnp.float32
BF16 = jnp.bfloat16

EPS = 1e-6
CHUNK = 128
GROUP_A = 128
DIFF_HEAD_DIM = 64
V_HEAD_DIM = 2 * DIFF_HEAD_DIM
CONV_WIDTH = 3
HALO = 8
LOG2E = 1.4426950408889634
NEG_BIG = -1e30

V7X_VMEM_LIMIT_BYTES = 60 * 2**20

FFN_ROWS = 1024
FFN_COLS = 512
FFN_X_SLICES = 8
ROW_TILE = 512
SCORE_LOOKAHEAD = 3


def _params(ndim):
    return pltpu.CompilerParams(dimension_semantics=("arbitrary",) * ndim,
                                vmem_limit_bytes=V7X_VMEM_LIMIT_BYTES)


def _dot(a, b):
    return jnp.dot(a, b, preferred_element_type=F32)


def _rms(xf, g):
    y = xf * lax.rsqrt(jnp.mean(xf * xf, axis=-1, keepdims=True) + EPS)
    return y * g


def _resident(block_shape, index_map):
    return pl.BlockSpec(block_shape, index_map, pipeline_mode=pl.Buffered(1))


def _norm_kernel(x_ref, g_ref, o_ref):
    o_ref[...] = _rms(x_ref[...], g_ref[...]).astype(o_ref.dtype)


def _norm(x, gains, layer):
    m, d = x.shape
    return pl.pallas_call(
        _norm_kernel,
        out_shape=jax.ShapeDtypeStruct((m, d), BF16),
        grid=(m // ROW_TILE,),
        in_specs=[pl.BlockSpec((ROW_TILE, d), lambda i: (i, 0)),
                  pl.BlockSpec((None, 1, d), lambda i: (layer, 0, 0))],
        out_specs=pl.BlockSpec((ROW_TILE, d), lambda i: (i, 0)),
        compiler_params=_params(1),
        name="rmsnorm_in",
    )(x, gains)


def _ffn_kernel(h_ref, xs_ref, wg_ref, wu_ref, wd_ref, gn_ref, xo_ref, hn_ref, *, nf, rows_xs):
    f = pl.program_id(1)

    @pl.when(f == 0)
    def _():
        xo_ref[...] = jnp.zeros_like(xo_ref)

    @pl.when(f < FFN_X_SLICES)
    def _():
        r0 = pl.multiple_of(f * rows_xs, rows_xs)
        xo_ref[pl.ds(r0, rows_xs), :] += xs_ref[...]

    h = h_ref[...]
    g = _dot(h, wg_ref[...])
    u = _dot(h, wu_ref[...])
    a = ((0.5 * g) * jax.nn.sigmoid(g) * u).astype(BF16)
    xo_ref[...] += _dot(a, wd_ref[...])

    @pl.when(f == nf - 1)
    def _():
        hn_ref[...] = _rms(xo_ref[...], gn_ref[...]).astype(hn_ref.dtype)


def _ffn(h, x, wg, wu, wd, gains, layer, gain_layer, *, rows, out_dtype):
    m, d = x.shape
    d_ff = wg.shape[-1]
    nf = d_ff // FFN_COLS
    assert nf >= FFN_X_SLICES and rows % (FFN_X_SLICES * 8) == 0
    rows_xs = rows // FFN_X_SLICES
    kern = functools.partial(_ffn_kernel, nf=nf, rows_xs=rows_xs)
    return pl.pallas_call(
        kern,
        out_shape=[jax.ShapeDtypeStruct((m, d), F32), jax.ShapeDtypeStruct((m, d), out_dtype)],
        grid=(m // rows, nf),
        in_specs=[
            pl.BlockSpec((rows, d), lambda i, f: (i, 0)),
            pl.BlockSpec((rows_xs, d),
                         lambda i, f: (i * FFN_X_SLICES + jnp.minimum(f, FFN_X_SLICES - 1), 0)),
            pl.BlockSpec((None, d, FFN_COLS), lambda i, f: (layer, 0, f)),
            pl.BlockSpec((None, d, FFN_COLS), lambda i, f: (layer, 0, f)),
            pl.BlockSpec((None, FFN_COLS, d), lambda i, f: (layer, f, 0)),
            pl.BlockSpec((None, 1, d), lambda i, f: (gain_layer, 0, 0)),
        ],
        out_specs=[pl.BlockSpec((rows, d), lambda i, f: (i, 0)),
                   pl.BlockSpec((rows, d), lambda i, f: (i, 0))],
        compiler_params=_params(2),
        name="ffn_swiglu",
    )(h, x, wg, wu, wd, gains)


def _inproj_kernel(h_ref, wuv_ref, wq_ref, wk_ref, wvt_ref, wc_ref, gv_ref, ws_ref, bs_ref, wconv_ref,
                   ya_ref, q_ref, k_ref, vt_ref, yc_ref, zs_ref, *, tiles_per_seq, q_scale):
    i = pl.program_id(0)
    rows = h_ref.shape[0]
    wa = ya_ref.shape[1]
    wcw = yc_ref.shape[1]
    h = h_ref[...]

    u = _dot(h, wuv_ref[:, 0:wa])
    v = _dot(h, wuv_ref[:, wa:2 * wa])
    vn = _rms(v, gv_ref[...]).astype(BF16)
    t_idx = lax.broadcasted_iota(jnp.int32, (CHUNK, CHUNK), 0)
    s_idx = lax.broadcasted_iota(jnp.int32, (CHUNK, CHUNK), 1)
    for g in range(wa // GROUP_A):
        cs = slice(g * GROUP_A, (g + 1) * GROUP_A)
        ws_g = jnp.where(t_idx >= s_idx, ws_ref[g], 0.0).astype(BF16)
        b_g = bs_ref[:, g:g + 1]
        for c in range(rows // CHUNK):
            rs = slice(c * CHUNK, (c + 1) * CHUNK)
            mixed = _dot(ws_g, vn[rs, cs]) + b_g
            ya_ref[rs, cs] = (u[rs, cs] * mixed).astype(ya_ref.dtype)

    q_ref[...] = (_dot(h, wq_ref[...]) * q_scale).astype(q_ref.dtype)
    k_ref[...] = _dot(h, wk_ref[...]).astype(k_ref.dtype)
    vt_ref[...] = lax.dot_general(wvt_ref[...], h, (((1,), (1,)), ((), ())),
                                  preferred_element_type=F32).astype(vt_ref.dtype)

    bg = _dot(h, wc_ref[:, 0:wcw])
    cg = _dot(h, wc_ref[:, wcw:2 * wcw])
    xc = _dot(h, wc_ref[:, 2 * wcw:3 * wcw])
    z = cg * xc

    @pl.when(i % tiles_per_seq == 0)
    def _():
        zs_ref[0:HALO, :] = jnp.zeros((HALO, wcw), F32)

    zs_ref[HALO:HALO + rows, :] = z
    w = wconv_ref[...]
    y = zs_ref[HALO - 2:HALO - 2 + rows, :] * w[0:1, :]
    y = y + zs_ref[HALO - 1:HALO - 1 + rows, :] * w[1:2, :]
    y = y + z * w[2:3, :]
    yc_ref[...] = (bg * y).astype(yc_ref.dtype)
    zs_ref[0:HALO, :] = zs_ref[rows:rows + HALO, :]


def _inproj(h, wuv, wq, wk, wvt, wc, gv, ws, bs_t, wconv, layer, *, seq):
    m, d = h.shape
    rows = ROW_TILE
    wa = wuv.shape[-1] // 2
    wb = wq.shape[-1]
    wcw = wc.shape[-1] // 3
    kern = functools.partial(_inproj_kernel, tiles_per_seq=seq // rows,
                             q_scale=DIFF_HEAD_DIM ** -0.5 * LOG2E)
    lw = lambda i: (layer, 0, 0)
    return pl.pallas_call(
        kern,
        out_shape=[jax.ShapeDtypeStruct((m, wa), BF16),
                   jax.ShapeDtypeStruct((m, wb), BF16),
                   jax.ShapeDtypeStruct((m, wb), BF16),
                   jax.ShapeDtypeStruct((m // rows, wb, rows), BF16),
                   jax.ShapeDtypeStruct((m, wcw), BF16)],
        grid=(m // rows,),
        in_specs=[
            pl.BlockSpec((rows, d), lambda i: (i, 0)),
            _resident((None, d, 2 * wa), lw),
            _resident((None, d, wb), lw),
            _resident((None, d, wb), lw),
            _resident((None, wb, d), lw),
            _resident((None, d, 3 * wcw), lw),
            pl.BlockSpec((None, 1, wa), lw),
            pl.BlockSpec((None, wa // GROUP_A, CHUNK, CHUNK), lambda i: (layer, 0, 0, 0)),
            pl.BlockSpec((None, CHUNK, wa // GROUP_A), lw),
            pl.BlockSpec((None, CONV_WIDTH, wcw), lw),
        ],
        out_specs=[pl.BlockSpec((rows, wa), lambda i: (i, 0)),
                   pl.BlockSpec((rows, wb), lambda i: (i, 0)),
                   pl.BlockSpec((rows, wb), lambda i: (i, 0)),
                   pl.BlockSpec((None, wb, rows), lambda i: (i, 0, 0)),
                   pl.BlockSpec((rows, wcw), lambda i: (i, 0))],
        scratch_shapes=[pltpu.VMEM((rows + HALO, wcw), F32)],
        compiler_params=_params(1),
        name="inproj_mix_ac",
    )(h, wuv, wq, wk, wvt, wc, gv, ws, bs_t, wconv)


def _split3(x):
    rnd = lambda v: v.astype(BF16).astype(F32)
    hi = rnd(x)
    mid = rnd(x - hi)
    lo = rnd(x - hi - mid)
    return hi, mid, lo


def _attn_kernel(lq_ref, gs_ref, q_ref, k_ref, vt_ref, o_ref, biasd_ref, kc_ref, qc_ref,
                 m_ref, l_ref, acc_ref, *, lambda_init):
    hd = pl.program_id(1)
    qi = pl.program_id(2)
    t = q_ref.shape[0]
    slope2 = jnp.exp2(-(jnp.full((1, t), hd + 1, jnp.int32).astype(F32))) * LOG2E

    @pl.when(qi == 0)
    def _():
        kk = lax.broadcasted_iota(jnp.int32, (t, t), 0)
        qq = lax.broadcasted_iota(jnp.int32, (t, t), 1)
        biasd_ref[...] = jnp.where(kk <= qq, (kk - qq).astype(F32) * slope2, NEG_BIG)
        r = lax.broadcasted_iota(jnp.int32, (t, V_HEAD_DIM), 0)
        col = jnp.bitwise_and(lax.broadcasted_iota(jnp.int32, (t, V_HEAD_DIM), 1), DIFF_HEAD_DIM - 1)
        rf = r.astype(F32)
        r_hi = jnp.bitwise_and(r, -16).astype(F32)
        kc = jnp.where(col < 3, r_hi, jnp.where(col < 6, rf - r_hi, jnp.where(col < 9, 1.0, 0.0)))
        kc_ref[...] = kc.astype(BF16)
        s_col = jnp.exp2(-(jnp.full((t, V_HEAD_DIM), hd + 1, jnp.int32).astype(F32))) * LOG2E
        s3 = _split3(s_col)
        w3 = _split3(-(s_col * rf))
        qc = jnp.zeros((t, V_HEAD_DIM), F32)
        for n in range(3):
            qc = jnp.where(col == n, s3[n], qc)
            qc = jnp.where(col == n + 3, s3[n], qc)
            qc = jnp.where(col == n + 6, w3[n], qc)
        qc_ref[...] = qc.astype(BF16)

    q = q_ref[...]
    lane = lax.broadcasted_iota(jnp.int32, q.shape, 1)
    own = (lane < DIFF_HEAD_DIM, lane >= DIFF_HEAD_DIM)
    zero = jnp.zeros_like(q)
    q_plain = [jnp.where(own[mp], q, zero) for mp in range(2)]
    q_bias = [jnp.where(own[mp], q, qc_ref[...]) for mp in range(2)]
    ones_rows = jnp.ones((16, t), BF16)

    def score(j, mp, diag):
        dims = (((1,), (1,)), ((), ()))
        if diag:
            return lax.dot_general(k_ref[j], q_plain[mp], dims, preferred_element_type=F32) + biasd_ref[...]
        k_aug = jnp.where(own[mp], k_ref[j], kc_ref[...])
        return lax.dot_general(k_aug, q_bias[mp], dims, preferred_element_type=F32)

    def update(state, s, j):
        m_old, l_old, a_old = state
        c = slope2 * ((j - qi) * t).astype(F32)
        m_new = jnp.maximum(m_old, jnp.max(s, axis=0, keepdims=True) + c)
        p = jnp.exp2(s - (m_new - c)).astype(BF16)
        alpha = jnp.exp2(m_old - m_new)
        pv = _dot(jnp.concatenate([vt_ref[j], ones_rows], axis=0), p)
        l_new = l_old * alpha + pv[V_HEAD_DIM:V_HEAD_DIM + 1, :]
        a_new = a_old * alpha + pv[0:V_HEAD_DIM, :]
        return m_new, l_new, a_new

    def group(j0, size, diag_last):
        def run():
            work = [(j0 + i, mp, diag_last and i == size - 1) for i in range(size) for mp in range(2)]
            states = [(m_ref[mp], l_ref[mp], acc_ref[mp]) for mp in range(2)]
            pending = [score(*w) for w in work[:SCORE_LOOKAHEAD]]
            for n, (j, mp, _) in enumerate(work):
                states[mp] = update(states[mp], pending[n], j)
                if n + SCORE_LOOKAHEAD < len(work):
                    pending.append(score(*work[n + SCORE_LOOKAHEAD]))
            for mp in range(2):
                m_ref[mp], l_ref[mp], acc_ref[mp] = states[mp]
        return run

    for mp in range(2):
        m_ref[mp] = jnp.full((1, t), NEG_BIG, F32)
        l_ref[mp] = jnp.zeros((1, t), F32)
        acc_ref[mp] = jnp.zeros((V_HEAD_DIM, t), F32)

    n_tiles = qi + 1
    rem = n_tiles % 4
    plain4 = n_tiles // 4 - (rem == 0).astype(jnp.int32)
    base = 4 * plain4
    base2 = base + 2 * (rem == 3).astype(jnp.int32)
    pl.when(plain4 == 1)(group(0, 4, False))
    pl.when(rem == 0)(group(base, 4, True))
    pl.when(rem == 3)(group(base, 2, False))
    pl.when(rem == 2)(group(base, 2, True))
    pl.when((rem == 1) | (rem == 3))(group(base2, 1, True))
    l1, a1, l2, a2 = l_ref[0], acc_ref[0], l_ref[1], acc_ref[1]

    lq = lq_ref[...]
    s01 = jnp.sum(lq[0:1, :] * lq[1:2, :], axis=1, keepdims=True)
    s23 = jnp.sum(lq[2:3, :] * lq[3:4, :], axis=1, keepdims=True)
    lam = jnp.exp(s01) - jnp.exp(s23) + lambda_init
    o = a1 * (1.0 / l1) - lam * (a2 * (1.0 / l2))
    y = o * lax.rsqrt(jnp.mean(o * o, axis=0, keepdims=True) + EPS)
    y = y * gs_ref[...] * (1.0 - lambda_init)
    o_ref[...] = y.T.astype(o_ref.dtype)


def _attn(q, k, vt, lq, gsub, layer, *, batch, seq, lambda_init):
    m, wb = q.shape
    t = ROW_TILE
    nq = seq // t
    heads = wb // V_HEAD_DIM
    k3 = k.reshape(m // t, t, wb)
    kern = functools.partial(_attn_kernel, lambda_init=lambda_init)
    return pl.pallas_call(
        kern,
        out_shape=jax.ShapeDtypeStruct((m, wb), BF16),
        grid=(batch, heads, nq),
        in_specs=[
            pl.BlockSpec((None, 4, DIFF_HEAD_DIM), lambda b, h, i: (layer, 0, 0)),
            pl.BlockSpec((None, V_HEAD_DIM, 1), lambda b, h, i: (layer, 0, 0)),
            pl.BlockSpec((t, V_HEAD_DIM), lambda b, h, i: (b * nq + i, h)),
            pl.BlockSpec((nq, t, V_HEAD_DIM), lambda b, h, i: (b, 0, h)),
            pl.BlockSpec((nq, V_HEAD_DIM, t), lambda b, h, i: (b, h, 0)),
        ],
        out_specs=pl.BlockSpec((t, V_HEAD_DIM), lambda b, h, i: (b * nq + i, h)),
        scratch_shapes=[pltpu.VMEM((t, t), F32), pltpu.VMEM((t, V_HEAD_DIM), BF16),
                        pltpu.VMEM((t, V_HEAD_DIM), BF16),
                        pltpu.VMEM((2, 1, t), F32), pltpu.VMEM((2, 1, t), F32),
                        pltpu.VMEM((2, V_HEAD_DIM, t), F32)],
        compiler_params=_params(3),
        name="diff_attn",
    )(lq, gsub, q, k3, vt)


def _outproj_kernel(ya_ref, yb_ref, yc_ref, x_ref, w_ref, gn_ref, xo_ref, hn_ref):
    wa = ya_ref.shape[1]
    wb = yb_ref.shape[1]
    acc = x_ref[...] + _dot(ya_ref[...], w_ref[0:wa, :])
    acc = acc + _dot(yb_ref[...], w_ref[wa:wa + wb, :])
    acc = acc + _dot(yc_ref[...], w_ref[wa + wb:, :])
    xo_ref[...] = acc
    hn_ref[...] = _rms(acc, gn_ref[...]).astype(hn_ref.dtype)


def _outproj(ya, yb, yc, x, w, gains, layer):
    m, d = x.shape
    rows = ROW_TILE
    row_block = lambda width: pl.BlockSpec((rows, width), lambda i: (i, 0))
    return pl.pallas_call(
        _outproj_kernel,
        out_shape=[jax.ShapeDtypeStruct((m, d), F32), jax.ShapeDtypeStruct((m, d), BF16)],
        grid=(m // rows,),
        in_specs=[row_block(ya.shape[1]), row_block(yb.shape[1]), row_block(yc.shape[1]), row_block(d),
                  _resident((None,) + w.shape[1:], lambda i: (layer, 0, 0)),
                  pl.BlockSpec((None, 1, d), lambda i: (layer, 0, 0))],
        out_specs=[row_block(d), row_block(d)],
        compiler_params=_params(1),
        name="outproj",
    )(ya, yb, yc, x, w, gains)


def kernel(x, g_ffn1, w_ffn1_gate, w_ffn1_up, w_ffn1_down, g_mix, w_in, g_sga_v, w_sga_s, b_sga_s,
           lambda_qk, g_diff_sub, w_conv, w_out, g_ffn2, w_ffn2_gate, w_ffn2_up, w_ffn2_down, g_final):
    batch, seq, d = x.shape
    depth = w_in.shape[0]
    wa = g_sga_v.shape[-1]
    wb = w_out.shape[1] - 2 * wa
    assert seq % ROW_TILE == 0 and wb % V_HEAD_DIM == 0

    cast = lambda w: w.astype(BF16)
    w1g, w1u, w1d = cast(w_ffn1_gate), cast(w_ffn1_up), cast(w_ffn1_down)
    w2g, w2u, w2d = cast(w_ffn2_gate), cast(w_ffn2_up), cast(w_ffn2_down)
    o = 2 * wa
    wuv = cast(w_in[:, :, 0:o])
    wq = cast(w_in[:, :, o:o + wb])
    wk = cast(w_in[:, :, o + wb:o + 2 * wb])
    wvt = cast(jnp.swapaxes(w_in[:, :, o + 2 * wb:o + 3 * wb], 1, 2))
    wc = cast(w_in[:, :, o + 3 * wb:])
    wo = cast(w_out)
    row = lambda g: g.reshape(g.shape[0], 1, g.shape[-1])
    g1, gm, g2, gv = row(g_ffn1), row(g_mix), row(g_ffn2), row(g_sga_v)
    gf = g_final.reshape(1, 1, d)
    bs_t = jnp.swapaxes(b_sga_s, 1, 2)
    gsub = g_diff_sub.reshape(depth, V_HEAD_DIM, 1)

    xf = x.reshape(batch * seq, d)
    h = _norm(xf, g1, 0)
    for l in range(depth):
        xf, h = _ffn(h, xf, w1g, w1u, w1d, gm, l, l, rows=FFN_ROWS, out_dtype=BF16)
        ya, q, k, vt, yc = _inproj(h, wuv, wq, wk, wvt, wc, gv, w_sga_s, bs_t, w_conv, l, seq=seq)
        lambda_init = 0.8 - 0.6 * math.exp(-0.3 * l)
        yb = _attn(q, k, vt, lambda_qk, gsub, l, batch=batch, seq=seq, lambda_init=lambda_init)
        xf, h = _outproj(ya, yb, yc, xf, wo, g2, l)
        if l + 1 < depth:
            xf, h = _ffn(h, xf, w2g, w2u, w2d, g1, l, l + 1, rows=FFN_ROWS, out_dtype=BF16)
        else:
            xf, h = _ffn(h, xf, w2g, w2u, w2d, gf, l, 0, rows=ROW_TILE, out_dtype=F32)
    return h.reshape(batch, seq, d)
```

```python
import functools
import math

import jax
import jax.numpy as jnp
from jax import lax
from jax.experimental import pallas as pl
from jax.experimental.pallas import tpu as pltpu

F32 = jnp.float32
BF16 = jnp.bfloat16

EPS = 1e-6
CHUNK = 128
GROUP_A = 128
DIFF_HEAD_DIM = 64
V_HEAD_DIM = 2 * DIFF_HEAD_DIM
CONV_WIDTH = 3
HALO = 8
LOG2E = 1.4426950408889634
NEG_BIG = -1e30

V7X_VMEM_LIMIT_BYTES = 60 * 2**20

FFN_ROWS = 1024
FFN_COLS = 512
FFN_X_SLICES = 8
NORM_ROWS = 256
ROW_TILE = 512
SCORE_LOOKAHEAD = 3


def _params(ndim):
    return pltpu.CompilerParams(dimension_semantics=("arbitrary",) * ndim,
                                vmem_limit_bytes=V7X_VMEM_LIMIT_BYTES)


def _dot(a, b):
    return jnp.dot(a, b, preferred_element_type=F32)


def _rms(xf, g):
    y = xf * lax.rsqrt(jnp.mean(xf * xf, axis=-1, keepdims=True) + EPS)
    return y * g


def _resident(block_shape, index_map):
    return pl.BlockSpec(block_shape, index_map, pipeline_mode=pl.Buffered(1))


def _norm_kernel(x_ref, g_ref, o_ref):
    o_ref[...] = _rms(x_ref[...], g_ref[...]).astype(o_ref.dtype)


def _norm(x, gains, layer):
    m, d = x.shape
    return pl.pallas_call(
        _norm_kernel,
        out_shape=jax.ShapeDtypeStruct((m, d), BF16),
        grid=(m // ROW_TILE,),
        in_specs=[pl.BlockSpec((ROW_TILE, d), lambda i: (i, 0)),
                  pl.BlockSpec((None, 1, d), lambda i: (layer, 0, 0))],
        out_specs=pl.BlockSpec((ROW_TILE, d), lambda i: (i, 0)),
        compiler_params=_params(1),
        name="rmsnorm_in",
    )(x, gains)


def _ffn_kernel(h_ref, xs_ref, wg_ref, wu_ref, wd_ref, gn_ref, *out_and_scratch, nf, rows_xs, final):
    if final:
        hn_ref, xo_ref = out_and_scratch
    else:
        xo_ref, hn_ref = out_and_scratch
    f = pl.program_id(1)
    rows = h_ref.shape[0]

    def act():
        h = h_ref[...]
        g = _dot(h, wg_ref[...])
        u = _dot(h, wu_ref[...])
        return ((0.5 * g) * jax.nn.sigmoid(g) * u).astype(BF16)

    @pl.when(f == 0)
    def _():
        dn = _dot(act(), wd_ref[...])
        xo_ref[0:rows_xs, :] = dn[0:rows_xs, :] + xs_ref[...]
        xo_ref[rows_xs:, :] = dn[rows_xs:, :]

    @pl.when((f > 0) & (f < nf - 1))
    def _():
        r0 = pl.multiple_of(jnp.minimum(f, FFN_X_SLICES - 1) * rows_xs, rows_xs)
        xo_ref[pl.ds(r0, rows_xs), :] += jnp.where(f < FFN_X_SLICES, xs_ref[...], 0.0)
        xo_ref[...] += _dot(act(), wd_ref[...])

    @pl.when(f == nf - 1)
    def _():
        a = act()
        for r0 in range(0, rows, NORM_ROWS):
            rs = slice(r0, r0 + NORM_ROWS)
            xr = xo_ref[rs, :] + _dot(a[rs, :], wd_ref[...])
            if not final:
                xo_ref[rs, :] = xr
            hn_ref[rs, :] = _rms(xr, gn_ref[...]).astype(hn_ref.dtype)


def _ffn(h, x, wg, wu, wd, gains, layer, gain_layer, *, rows, final):
    m, d = x.shape
    d_ff = wg.shape[-1]
    nf = d_ff // FFN_COLS
    assert nf > FFN_X_SLICES and rows % (FFN_X_SLICES * 8) == 0 and rows % NORM_ROWS == 0
    rows_xs = rows // FFN_X_SLICES
    kern = functools.partial(_ffn_kernel, nf=nf, rows_xs=rows_xs, final=final)
    row_block = pl.BlockSpec((rows, d), lambda i, f: (i, 0))
    if final:
        out_shape = jax.ShapeDtypeStruct((m, d), F32)
        out_specs = row_block
        scratch = [pltpu.VMEM((rows, d), F32)]
    else:
        out_shape = [jax.ShapeDtypeStruct((m, d), F32), jax.ShapeDtypeStruct((m, d), BF16)]
        out_specs = [row_block, row_block]
        scratch = []
    return pl.pallas_call(
        kern,
        out_shape=out_shape,
        grid=(m // rows, nf),
        in_specs=[
            row_block,
            pl.BlockSpec((rows_xs, d),
                         lambda i, f: (i * FFN_X_SLICES + jnp.minimum(f, FFN_X_SLICES - 1), 0)),
            pl.BlockSpec((None, d, FFN_COLS), lambda i, f: (layer, 0, f)),
            pl.BlockSpec((None, d, FFN_COLS), lambda i, f: (layer, 0, f)),
            pl.BlockSpec((None, FFN_COLS, d), lambda i, f: (layer, f, 0)),
            pl.BlockSpec((None, 1, d), lambda i, f: (gain_layer, 0, 0)),
        ],
        out_specs=out_specs,
        scratch_shapes=scratch,
        compiler_params=_params(2),
        name="ffn_swiglu_final" if final else "ffn_swiglu",
    )(h, x, wg, wu, wd, gains)


def _inproj_kernel(h_ref, wuv_ref, wq_ref, wk_ref, wvt_ref, wc_ref, gv_ref, ws_ref, bs_ref, wconv_ref,
                   ya_ref, q_ref, k_ref, vt_ref, yc_ref, zs_ref, *, tiles_per_seq, q_scale):
    i = pl.program_id(0)
    rows = h_ref.shape[0]
    wa = ya_ref.shape[1]
    wcw = yc_ref.shape[1]
    h = h_ref[...]

    u = _dot(h, wuv_ref[:, 0:wa])
    v = _dot(h, wuv_ref[:, wa:2 * wa])
    vn = _rms(v, gv_ref[...]).astype(BF16)
    t_idx = lax.broadcasted_iota(jnp.int32, (CHUNK, CHUNK), 0)
    s_idx = lax.broadcasted_iota(jnp.int32, (CHUNK, CHUNK), 1)
    for g in range(wa // GROUP_A):
        cs = slice(g * GROUP_A, (g + 1) * GROUP_A)
        ws_g = jnp.where(t_idx >= s_idx, ws_ref[g], 0.0).astype(BF16)
        b_g = bs_ref[:, g:g + 1]
        for c in range(rows // CHUNK):
            rs = slice(c * CHUNK, (c + 1) * CHUNK)
            mixed = _dot(ws_g, vn[rs, cs]) + b_g
            ya_ref[rs, cs] = (u[rs, cs] * mixed).astype(ya_ref.dtype)

    q_ref[...] = (_dot(h, wq_ref[...]) * q_scale).astype(q_ref.dtype)
    k_ref[...] = _dot(h, wk_ref[...]).astype(k_ref.dtype)
    vt_ref[...] = lax.dot_general(wvt_ref[...], h, (((1,), (1,)), ((), ())),
                                  preferred_element_type=F32).astype(vt_ref.dtype)

    bg = _dot(h, wc_ref[:, 0:wcw])
    cg = _dot(h, wc_ref[:, wcw:2 * wcw])
    xc = _dot(h, wc_ref[:, 2 * wcw:3 * wcw])
    z = cg * xc

    @pl.when(i % tiles_per_seq == 0)
    def _():
        zs_ref[0:HALO, :] = jnp.zeros((HALO, wcw), F32)

    zs_ref[HALO:HALO + rows, :] = z
    w = wconv_ref[...]
    y = zs_ref[HALO - 2:HALO - 2 + rows, :] * w[0:1, :]
    y = y + zs_ref[HALO - 1:HALO - 1 + rows, :] * w[1:2, :]
    y = y + z * w[2:3, :]
    yc_ref[...] = (bg * y).astype(yc_ref.dtype)
    zs_ref[0:HALO, :] = zs_ref[rows:rows + HALO, :]


def _inproj(h, wuv, wq, wk, wvt, wc, gv, ws, bs_t, wconv, layer, *, seq):
    m, d = h.shape
    rows = ROW_TILE
    wa = wuv.shape[-1] // 2
    wb = wq.shape[-1]
    wcw = wc.shape[-1] // 3
    kern = functools.partial(_inproj_kernel, tiles_per_seq=seq // rows,
                             q_scale=DIFF_HEAD_DIM ** -0.5 * LOG2E)
    lw = lambda i: (layer, 0, 0)
    return pl.pallas_call(
        kern,
        out_shape=[jax.ShapeDtypeStruct((m, wa), BF16),
                   jax.ShapeDtypeStruct((m, wb), BF16),
                   jax.ShapeDtypeStruct((m, wb), BF16),
                   jax.ShapeDtypeStruct((m // rows, wb, rows), BF16),
                   jax.ShapeDtypeStruct((m, wcw), BF16)],
        grid=(m // rows,),
        in_specs=[
            pl.BlockSpec((rows, d), lambda i: (i, 0)),
            _resident((None, d, 2 * wa), lw),
            _resident((None, d, wb), lw),
            _resident((None, d, wb), lw),
            _resident((None, wb, d), lw),
            _resident((None, d, 3 * wcw), lw),
            pl.BlockSpec((None, 1, wa), lw),
            pl.BlockSpec((None, wa // GROUP_A, CHUNK, CHUNK), lambda i: (layer, 0, 0, 0)),
            pl.BlockSpec((None, CHUNK, wa // GROUP_A), lw),
            pl.BlockSpec((None, CONV_WIDTH, wcw), lw),
        ],
        out_specs=[pl.BlockSpec((rows, wa), lambda i: (i, 0)),
                   pl.BlockSpec((rows, wb), lambda i: (i, 0)),
                   pl.BlockSpec((rows, wb), lambda i: (i, 0)),
                   pl.BlockSpec((None, wb, rows), lambda i: (i, 0, 0)),
                   pl.BlockSpec((rows, wcw), lambda i: (i, 0))],
        scratch_shapes=[pltpu.VMEM((rows + HALO, wcw), F32)],
        compiler_params=_params(1),
        name="inproj_mix_ac",
    )(h, wuv, wq, wk, wvt, wc, gv, ws, bs_t, wconv)


def _split3(x):
    rnd = lambda v: v.astype(BF16).astype(F32)
    hi = rnd(x)
    mid = rnd(x - hi)
    lo = rnd(x - hi - mid)
    return hi, mid, lo


def _attn_kernel(lq_ref, gs_ref, q_ref, k_ref, vt_ref, o_ref, biasd_ref, kc_ref, qc_ref, *, lambda_init):
    hd = pl.program_id(1)
    qi = pl.program_id(2)
    t = q_ref.shape[0]
    slope2 = jnp.exp2(-(jnp.full((1, t), hd + 1, jnp.int32).astype(F32))) * LOG2E

    @pl.when(qi == 0)
    def _():
        kk = lax.broadcasted_iota(jnp.int32, (t, t), 0)
        qq = lax.broadcasted_iota(jnp.int32, (t, t), 1)
        biasd_ref[...] = jnp.where(kk <= qq, (kk - qq).astype(F32) * slope2, NEG_BIG)
        r = lax.broadcasted_iota(jnp.int32, (t, V_HEAD_DIM), 0)
        col = jnp.bitwise_and(lax.broadcasted_iota(jnp.int32, (t, V_HEAD_DIM), 1), DIFF_HEAD_DIM - 1)
        rf = r.astype(F32)
        r_hi = jnp.bitwise_and(r, -16).astype(F32)
        kc = jnp.where(col < 3, r_hi, jnp.where(col < 6, rf - r_hi, jnp.where(col < 9, 1.0, 0.0)))
        kc_ref[...] = kc.astype(BF16)
        s_col = jnp.exp2(-(jnp.full((t, V_HEAD_DIM), hd + 1, jnp.int32).astype(F32))) * LOG2E
        s3 = _split3(s_col)
        w3 = _split3(-(s_col * rf))
        qc = jnp.zeros((t, V_HEAD_DIM), F32)
        for n in range(3):
            qc = jnp.where(col == n, s3[n], qc)
            qc = jnp.where(col == n + 3, s3[n], qc)
            qc = jnp.where(col == n + 6, w3[n], qc)
        qc_ref[...] = qc.astype(BF16)

    q = q_ref[...]
    lane = lax.broadcasted_iota(jnp.int32, q.shape, 1)
    own = (lane < DIFF_HEAD_DIM, lane >= DIFF_HEAD_DIM)
    zero = jnp.zeros_like(q)
    q_plain = [jnp.where(own[mp], q, zero) for mp in range(2)]
    q_bias = [jnp.where(own[mp], q, qc_ref[...]) for mp in range(2)]
    ones_rows = jnp.ones((16, t), BF16)

    def score(j, mp, diag):
        dims = (((1,), (1,)), ((), ()))
        if diag:
            return lax.dot_general(k_ref[j], q_plain[mp], dims, preferred_element_type=F32) + biasd_ref[...]
        k_aug = jnp.where(own[mp], k_ref[j], kc_ref[...])
        return lax.dot_general(k_aug, q_bias[mp], dims, preferred_element_type=F32)

    def update(state, s, j):
        m_old, l_old, a_old = state
        c = slope2 * ((j - qi) * t).astype(F32)
        m_new = jnp.maximum(m_old, jnp.max(s, axis=0, keepdims=True) + c)
        p = jnp.exp2(s - (m_new - c)).astype(BF16)
        alpha = jnp.exp2(m_old - m_new)
        pv = _dot(jnp.concatenate([vt_ref[j], ones_rows], axis=0), p)
        l_new = l_old * alpha + pv[V_HEAD_DIM:V_HEAD_DIM + 1, :]
        a_new = a_old * alpha + pv[0:V_HEAD_DIM, :]
        return m_new, l_new, a_new

    def finish(states):
        (_, l1, a1), (_, l2, a2) = states
        lq = lq_ref[...]
        s01 = jnp.sum(lq[0:1, :] * lq[1:2, :], axis=1, keepdims=True)
        s23 = jnp.sum(lq[2:3, :] * lq[3:4, :], axis=1, keepdims=True)
        lam = jnp.exp(s01) - jnp.exp(s23) + lambda_init
        o = a1 * (1.0 / l1) - lam * (a2 * (1.0 / l2))
        y = o * lax.rsqrt(jnp.mean(o * o, axis=0, keepdims=True) + EPS)
        y = y * gs_ref[...] * (1.0 - lambda_init)
        o_ref[...] = y.T.astype(o_ref.dtype)

    def q_tile(n_tiles):
        def run():
            init = (jnp.full((1, t), NEG_BIG, F32), jnp.zeros((1, t), F32), jnp.zeros((V_HEAD_DIM, t), F32))
            states = [init, init]
            work = [(j, mp, j == n_tiles - 1) for j in range(n_tiles) for mp in range(2)]
            pending = [score(*w) for w in work[:SCORE_LOOKAHEAD]]
            for n, (j, mp, _) in enumerate(work):
                states[mp] = update(states[mp], pending[n], j)
                if n + SCORE_LOOKAHEAD < len(work):
                    pending.append(score(*work[n + SCORE_LOOKAHEAD]))
            finish(states)
        return run

    for n_tiles in range(1, k_ref.shape[0] + 1):
        pl.when(qi == n_tiles - 1)(q_tile(n_tiles))


def _attn(q, k, vt, lq, gsub, layer, *, batch, seq, lambda_init):
    m, wb = q.shape
    t = ROW_TILE
    nq = seq // t
    heads = wb // V_HEAD_DIM
    k3 = k.reshape(m // t, t, wb)
    kern = functools.partial(_attn_kernel, lambda_init=lambda_init)
    return pl.pallas_call(
        kern,
        out_shape=jax.ShapeDtypeStruct((m, wb), BF16),
        grid=(batch, heads, nq),
        in_specs=[
            pl.BlockSpec((None, 4, DIFF_HEAD_DIM), lambda b, h, i: (layer, 0, 0)),
            pl.BlockSpec((None, V_HEAD_DIM, 1), lambda b, h, i: (layer, 0, 0)),
            pl.BlockSpec((t, V_HEAD_DIM), lambda b, h, i: (b * nq + i, h)),
            pl.BlockSpec((nq, t, V_HEAD_DIM), lambda b, h, i: (b, 0, h)),
            pl.BlockSpec((nq, V_HEAD_DIM, t), lambda b, h, i: (b, h, 0)),
        ],
        out_specs=pl.BlockSpec((t, V_HEAD_DIM), lambda b, h, i: (b * nq + i, h)),
        scratch_shapes=[pltpu.VMEM((t, t), F32), pltpu.VMEM((t, V_HEAD_DIM), BF16),
                        pltpu.VMEM((t, V_HEAD_DIM), BF16)],
        compiler_params=_params(3),
        name="diff_attn",
    )(lq, gsub, q, k3, vt)


def _outproj_kernel(ya_ref, yb_ref, yc_ref, x_ref, w_ref, gn_ref, xo_ref, hn_ref):
    wa = ya_ref.shape[1]
    wb = yb_ref.shape[1]
    for r0 in range(0, x_ref.shape[0], NORM_ROWS):
        rs = slice(r0, r0 + NORM_ROWS)
        acc = x_ref[rs, :] + _dot(ya_ref[rs, :], w_ref[0:wa, :])
        acc = acc + _dot(yb_ref[rs, :], w_ref[wa:wa + wb, :])
        acc = acc + _dot(yc_ref[rs, :], w_ref[wa + wb:, :])
        xo_ref[rs, :] = acc
        hn_ref[rs, :] = _rms(acc, gn_ref[...]).astype(hn_ref.dtype)


def _outproj(ya, yb, yc, x, w, gains, layer):
    m, d = x.shape
    rows = ROW_TILE
    row_block = lambda width: pl.BlockSpec((rows, width), lambda i: (i, 0))
    return pl.pallas_call(
        _outproj_kernel,
        out_shape=[jax.ShapeDtypeStruct((m, d), F32), jax.ShapeDtypeStruct((m, d), BF16)],
        grid=(m // rows,),
        in_specs=[row_block(ya.shape[1]), row_block(yb.shape[1]), row_block(yc.shape[1]), row_block(d),
                  _resident((None,) + w.shape[1:], lambda i: (layer, 0, 0)),
                  pl.BlockSpec((None, 1, d), lambda i: (layer, 0, 0))],
        out_specs=[row_block(d), row_block(d)],
        compiler_params=_params(1),
        name="outproj",
    )(ya, yb, yc, x, w, gains)


def kernel(x, g_ffn1, w_ffn1_gate, w_ffn1_up, w_ffn1_down, g_mix, w_in, g_sga_v, w_sga_s, b_sga_s,
           lambda_qk, g_diff_sub, w_conv, w_out, g_ffn2, w_ffn2_gate, w_ffn2_up, w_ffn2_down, g_final):
    batch, seq, d = x.shape
    depth = w_in.shape[0]
    wa = g_sga_v.shape[-1]
    wb = w_out.shape[1] - 2 * wa
    assert seq % ROW_TILE == 0 and wb % V_HEAD_DIM == 0

    cast = lambda w: w.astype(BF16)
    w1g, w1u, w1d = cast(w_ffn1_gate), cast(w_ffn1_up), cast(w_ffn1_down)
    w2g, w2u, w2d = cast(w_ffn2_gate), cast(w_ffn2_up), cast(w_ffn2_down)
    o = 2 * wa
    wuv = cast(w_in[:, :, 0:o])
    wq = cast(w_in[:, :, o:o + wb])
    wk = cast(w_in[:, :, o + wb:o + 2 * wb])
    wvt = cast(jnp.swapaxes(w_in[:, :, o + 2 * wb:o + 3 * wb], 1, 2))
    wc = cast(w_in[:, :, o + 3 * wb:])
    wo = cast(w_out)
    row = lambda g: g.reshape(g.shape[0], 1, g.shape[-1])
    g1, gm, g2, gv = row(g_ffn1), row(g_mix), row(g_ffn2), row(g_sga_v)
    gf = g_final.reshape(1, 1, d)
    bs_t = jnp.swapaxes(b_sga_s, 1, 2)
    gsub = g_diff_sub.reshape(depth, V_HEAD_DIM, 1)

    xf = x.reshape(batch * seq, d)
    h = _norm(xf, g1, 0)
    for l in range(depth):
        xf, h = _ffn(h, xf, w1g, w1u, w1d, gm, l, l, rows=FFN_ROWS, final=False)
        ya, q, k, vt, yc = _inproj(h, wuv, wq, wk, wvt, wc, gv, w_sga_s, bs_t, w_conv, l, seq=seq)
        lambda_init = 0.8 - 0.6 * math.exp(-0.3 * l)
        yb = _attn(q, k, vt, lambda_qk, gsub, l, batch=batch, seq=seq, lambda_init=lambda_init)
        xf, h = _outproj(ya, yb, yc, xf, wo, g2, l)
        if l + 1 < depth:
            xf, h = _ffn(h, xf, w2g, w2u, w2d, g1, l, l + 1, rows=FFN_ROWS, final=False)
        else:
            out = _ffn(h, xf, w2g, w2u, w2d, gf, l, 0, rows=FFN_ROWS, final=True)
    return out.reshape(batch, seq, d)
```

```python
import functools
import math

import jax
import jax.numpy as jnp
from jax import lax
from jax.experimental import pallas as pl
from jax.experimental.pallas import tpu as pltpu

F32 = jnp.float32
BF16 = jnp.bfloat16

EPS = 1e-6
CHUNK = 128
GROUP_A = 128
DIFF_HEAD_DIM = 64
V_HEAD_DIM = 2 * DIFF_HEAD_DIM
CONV_WIDTH = 3
HALO = 8
LOG2E = 1.4426950408889634
NEG_BIG = -1e30

V7X_VMEM_LIMIT_BYTES = 60 * 2**20

FFN_ROWS = 1024
FFN_COLS = 512
FFN_X_SLICES = 8
NORM_ROWS = 256
ROW_TILE = 512
SCORE_LOOKAHEAD = 3


def _params(ndim):
    return pltpu.CompilerParams(dimension_semantics=("arbitrary",) * ndim,
                                vmem_limit_bytes=V7X_VMEM_LIMIT_BYTES)


def _dot(a, b):
    return jnp.dot(a, b, preferred_element_type=F32)


def _rms(xf, g):
    y = xf * lax.rsqrt(jnp.mean(xf * xf, axis=-1, keepdims=True) + EPS)
    return y * g


def _resident(block_shape, index_map):
    return pl.BlockSpec(block_shape, index_map, pipeline_mode=pl.Buffered(1))


def _norm_kernel(x_ref, g_ref, o_ref):
    o_ref[...] = _rms(x_ref[...], g_ref[...]).astype(o_ref.dtype)


def _norm(x, gains, layer):
    m, d = x.shape
    return pl.pallas_call(
        _norm_kernel,
        out_shape=jax.ShapeDtypeStruct((m, d), BF16),
        grid=(m // ROW_TILE,),
        in_specs=[pl.BlockSpec((ROW_TILE, d), lambda i: (i, 0)),
                  pl.BlockSpec((None, 1, d), lambda i: (layer, 0, 0))],
        out_specs=pl.BlockSpec((ROW_TILE, d), lambda i: (i, 0)),
        compiler_params=_params(1),
        name="rmsnorm_in",
    )(x, gains)


def _ffn_kernel(h_ref, xs_ref, wg_ref, wu_ref, wd_ref, gn_ref, *refs, nf, rows_xs, final):
    if final:
        hn_ref, xo_ref = refs
    else:
        next_f32 = refs[0:3]
        xo_ref, hn_ref = refs[3:5]
        next_bf16 = refs[5:8]
        for src_ref, dst_ref in zip(next_f32, next_bf16):
            dst_ref[...] = src_ref[...].astype(dst_ref.dtype)
    f = pl.program_id(1)

    @pl.when(f == 0)
    def _():
        xo_ref[...] = jnp.zeros_like(xo_ref)

    @pl.when(f < FFN_X_SLICES)
    def _():
        r0 = pl.multiple_of(f * rows_xs, rows_xs)
        xo_ref[pl.ds(r0, rows_xs), :] += xs_ref[...]

    h = h_ref[...]
    g = _dot(h, wg_ref[...])
    u = _dot(h, wu_ref[...])
    a = ((0.5 * g) * jax.nn.sigmoid(g) * u).astype(BF16)
    xo_ref[...] += _dot(a, wd_ref[...])

    @pl.when(f == nf - 1)
    def _():
        hn_ref[...] = _rms(xo_ref[...], gn_ref[...]).astype(hn_ref.dtype)


def _ffn(h, x, weights, gains, gain_layer, *, next_weights=None):
    m, d = x.shape
    wg, wu, wd = weights
    d_ff = wg.shape[-1]
    rows = FFN_ROWS
    nf = d_ff // FFN_COLS
    n_i = m // rows
    final = next_weights is None
    assert nf >= FFN_X_SLICES and rows % (FFN_X_SLICES * 8) == 0
    rows_xs = rows // FFN_X_SLICES
    kern = functools.partial(_ffn_kernel, nf=nf, rows_xs=rows_xs, final=final)
    row_block = pl.BlockSpec((rows, d), lambda i, f: (i, 0))
    in_specs = [
        row_block,
        pl.BlockSpec((rows_xs, d),
                     lambda i, f: (i * FFN_X_SLICES + jnp.minimum(f, FFN_X_SLICES - 1), 0)),
        pl.BlockSpec((d, FFN_COLS), lambda i, f: (0, f)),
        pl.BlockSpec((d, FFN_COLS), lambda i, f: (0, f)),
        pl.BlockSpec((FFN_COLS, d), lambda i, f: (f, 0)),
        pl.BlockSpec((None, 1, d), lambda i, f: (gain_layer, 0, 0)),
    ]
    operands = [h, x, wg, wu, wd, gains]
    if final:
        out_shape = jax.ShapeDtypeStruct((m, d), F32)
        out_specs = row_block
        scratch = [pltpu.VMEM((rows, d), F32)]
    else:
        ng, nu, nd, nl = next_weights
        up_rows = d // n_i
        down_rows = d_ff // (n_i * nf)
        assert up_rows * n_i == d and down_rows * n_i * nf == d_ff
        assert up_rows % 16 == 0 and down_rows % 16 == 0
        in_specs += [pl.BlockSpec((None, up_rows, FFN_COLS), lambda i, f: (nl, i, f)),
                     pl.BlockSpec((None, up_rows, FFN_COLS), lambda i, f: (nl, i, f)),
                     pl.BlockSpec((None, down_rows, d), lambda i, f: (nl, i * nf + f, 0))]
        operands += [ng, nu, nd]
        out_shape = [jax.ShapeDtypeStruct((m, d), F32), jax.ShapeDtypeStruct((m, d), BF16),
                     jax.ShapeDtypeStruct((d, d_ff), BF16), jax.ShapeDtypeStruct((d, d_ff), BF16),
                     jax.ShapeDtypeStruct((d_ff, d), BF16)]
        out_specs = [row_block, row_block,
                     pl.BlockSpec((up_rows, FFN_COLS), lambda i, f: (i, f)),
                     pl.BlockSpec((up_rows, FFN_COLS), lambda i, f: (i, f)),
                     pl.BlockSpec((down_rows, d), lambda i, f: (i * nf + f, 0))]
        scratch = []
    return pl.pallas_call(
        kern,
        out_shape=out_shape,
        grid=(n_i, nf),
        in_specs=in_specs,
        out_specs=out_specs,
        scratch_shapes=scratch,
        compiler_params=_params(2),
        name="ffn_swiglu_final" if final else "ffn_swiglu",
    )(*operands)


def _inproj_kernel(h_ref, wuv_ref, wq_ref, wk_ref, wvt_ref, wc_ref, gv_ref, ws_ref, bs_ref, wconv_ref,
                   ya_ref, q_ref, k_ref, vt_ref, yc_ref, zs_ref, *, tiles_per_seq, q_scale):
    i = pl.program_id(0)
    rows = h_ref.shape[0]
    wa = ya_ref.shape[1]
    wcw = yc_ref.shape[1]
    h = h_ref[...]

    u = _dot(h, wuv_ref[:, 0:wa])
    v = _dot(h, wuv_ref[:, wa:2 * wa])
    vn = _rms(v, gv_ref[...]).astype(BF16)
    t_idx = lax.broadcasted_iota(jnp.int32, (CHUNK, CHUNK), 0)
    s_idx = lax.broadcasted_iota(jnp.int32, (CHUNK, CHUNK), 1)
    for g in range(wa // GROUP_A):
        cs = slice(g * GROUP_A, (g + 1) * GROUP_A)
        ws_g = jnp.where(t_idx >= s_idx, ws_ref[g], 0.0).astype(BF16)
        b_g = bs_ref[:, g:g + 1]
        for c in range(rows // CHUNK):
            rs = slice(c * CHUNK, (c + 1) * CHUNK)
            mixed = _dot(ws_g, vn[rs, cs]) + b_g
            ya_ref[rs, cs] = (u[rs, cs] * mixed).astype(ya_ref.dtype)

    q_ref[...] = (_dot(h, wq_ref[...]) * q_scale).astype(q_ref.dtype)
    k_ref[...] = _dot(h, wk_ref[...]).astype(k_ref.dtype)
    vt_ref[...] = lax.dot_general(wvt_ref[...], h, (((1,), (1,)), ((), ())),
                                  preferred_element_type=F32).astype(vt_ref.dtype)

    bg = _dot(h, wc_ref[:, 0:wcw])
    cg = _dot(h, wc_ref[:, wcw:2 * wcw])
    xc = _dot(h, wc_ref[:, 2 * wcw:3 * wcw])
    z = cg * xc

    @pl.when(i % tiles_per_seq == 0)
    def _():
        zs_ref[0:HALO, :] = jnp.zeros((HALO, wcw), F32)

    zs_ref[HALO:HALO + rows, :] = z
    w = wconv_ref[...]
    y = zs_ref[HALO - 2:HALO - 2 + rows, :] * w[0:1, :]
    y = y + zs_ref[HALO - 1:HALO - 1 + rows, :] * w[1:2, :]
    y = y + z * w[2:3, :]
    yc_ref[...] = (bg * y).astype(yc_ref.dtype)
    zs_ref[0:HALO, :] = zs_ref[rows:rows + HALO, :]


def _inproj(h, wuv, wq, wk, wvt, wc, gv, ws, bs_t, wconv, layer, *, seq):
    m, d = h.shape
    rows = ROW_TILE
    wa = wuv.shape[-1] // 2
    wb = wq.shape[-1]
    wcw = wc.shape[-1] // 3
    kern = functools.partial(_inproj_kernel, tiles_per_seq=seq // rows,
                             q_scale=DIFF_HEAD_DIM ** -0.5 * LOG2E)
    lw = lambda i: (layer, 0, 0)
    return pl.pallas_call(
        kern,
        out_shape=[jax.ShapeDtypeStruct((m, wa), BF16),
                   jax.ShapeDtypeStruct((m, wb), BF16),
                   jax.ShapeDtypeStruct((m, wb), BF16),
                   jax.ShapeDtypeStruct((m // rows, wb, rows), BF16),
                   jax.ShapeDtypeStruct((m, wcw), BF16)],
        grid=(m // rows,),
        in_specs=[
            pl.BlockSpec((rows, d), lambda i: (i, 0)),
            _resident((None, d, 2 * wa), lw),
            _resident((None, d, wb), lw),
            _resident((None, d, wb), lw),
            _resident((None, wb, d), lw),
            _resident((None, d, 3 * wcw), lw),
            pl.BlockSpec((None, 1, wa), lw),
            pl.BlockSpec((None, wa // GROUP_A, CHUNK, CHUNK), lambda i: (layer, 0, 0, 0)),
            pl.BlockSpec((None, CHUNK, wa // GROUP_A), lw),
            pl.BlockSpec((None, CONV_WIDTH, wcw), lw),
        ],
        out_specs=[pl.BlockSpec((rows, wa), lambda i: (i, 0)),
                   pl.BlockSpec((rows, wb), lambda i: (i, 0)),
                   pl.BlockSpec((rows, wb), lambda i: (i, 0)),
                   pl.BlockSpec((None, wb, rows), lambda i: (i, 0, 0)),
                   pl.BlockSpec((rows, wcw), lambda i: (i, 0))],
        scratch_shapes=[pltpu.VMEM((rows + HALO, wcw), F32)],
        compiler_params=_params(1),
        name="inproj_mix_ac",
    )(h, wuv, wq, wk, wvt, wc, gv, ws, bs_t, wconv)


def _split3(x):
    rnd = lambda v: v.astype(BF16).astype(F32)
    hi = rnd(x)
    mid = rnd(x - hi)
    lo = rnd(x - hi - mid)
    return hi, mid, lo


def _attn_kernel(lq_ref, gs_ref, q_ref, k_ref, vt_ref, o_ref, biasd_ref, kc_ref, qc_ref, *, lambda_init):
    hd = pl.program_id(1)
    qi = pl.program_id(2)
    t = q_ref.shape[0]
    slope2 = jnp.exp2(-(jnp.full((1, t), hd + 1, jnp.int32).astype(F32))) * LOG2E

    @pl.when(qi == 0)
    def _():
        kk = lax.broadcasted_iota(jnp.int32, (t, t), 0)
        qq = lax.broadcasted_iota(jnp.int32, (t, t), 1)
        biasd_ref[...] = jnp.where(kk <= qq, (kk - qq).astype(F32) * slope2, NEG_BIG)
        r = lax.broadcasted_iota(jnp.int32, (t, V_HEAD_DIM), 0)
        col = jnp.bitwise_and(lax.broadcasted_iota(jnp.int32, (t, V_HEAD_DIM), 1), DIFF_HEAD_DIM - 1)
        rf = r.astype(F32)
        r_hi = jnp.bitwise_and(r, -16).astype(F32)
        kc = jnp.where(col < 3, r_hi, jnp.where(col < 6, rf - r_hi, jnp.where(col < 9, 1.0, 0.0)))
        kc_ref[...] = kc.astype(BF16)
        s_col = jnp.exp2(-(jnp.full((t, V_HEAD_DIM), hd + 1, jnp.int32).astype(F32))) * LOG2E
        s3 = _split3(s_col)
        w3 = _split3(-(s_col * rf))
        qc = jnp.zeros((t, V_HEAD_DIM), F32)
        for n in range(3):
            qc = jnp.where(col == n, s3[n], qc)
            qc = jnp.where(col == n + 3, s3[n], qc)
            qc = jnp.where(col == n + 6, w3[n], qc)
        qc_ref[...] = qc.astype(BF16)

    q = q_ref[...]
    lane = lax.broadcasted_iota(jnp.int32, q.shape, 1)
    own = (lane < DIFF_HEAD_DIM, lane >= DIFF_HEAD_DIM)
    zero = jnp.zeros_like(q)
    q_plain = [jnp.where(own[mp], q, zero) for mp in range(2)]
    q_bias = [jnp.where(own[mp], q, qc_ref[...]) for mp in range(2)]
    ones_rows = jnp.ones((16, t), BF16)

    def score(j, mp, diag):
        dims = (((1,), (1,)), ((), ()))
        if diag:
            return lax.dot_general(k_ref[j], q_plain[mp], dims, preferred_element_type=F32) + biasd_ref[...]
        k_aug = jnp.where(own[mp], k_ref[j], kc_ref[...])
        return lax.dot_general(k_aug, q_bias[mp], dims, preferred_element_type=F32)

    def update(state, s, j):
        m_old, l_old, a_old = state
        c = slope2 * ((j - qi) * t).astype(F32)
        m_new = jnp.maximum(m_old, jnp.max(s, axis=0, keepdims=True) + c)
        p = jnp.exp2(s - (m_new - c)).astype(BF16)
        alpha = jnp.exp2(m_old - m_new)
        pv = _dot(jnp.concatenate([vt_ref[j], ones_rows], axis=0), p)
        l_new = l_old * alpha + pv[V_HEAD_DIM:V_HEAD_DIM + 1, :]
        a_new = a_old * alpha + pv[0:V_HEAD_DIM, :]
        return m_new, l_new, a_new

    def finish(states):
        (_, l1, a1), (_, l2, a2) = states
        lq = lq_ref[...]
        s01 = jnp.sum(lq[0:1, :] * lq[1:2, :], axis=1, keepdims=True)
        s23 = jnp.sum(lq[2:3, :] * lq[3:4, :], axis=1, keepdims=True)
        lam = jnp.exp(s01) - jnp.exp(s23) + lambda_init
        o = a1 * (1.0 / l1) - lam * (a2 * (1.0 / l2))
        y = o * lax.rsqrt(jnp.mean(o * o, axis=0, keepdims=True) + EPS)
        y = y * gs_ref[...] * (1.0 - lambda_init)
        o_ref[...] = y.T.astype(o_ref.dtype)

    def q_tile(n_tiles):
        def run():
            init = (jnp.full((1, t), NEG_BIG, F32), jnp.zeros((1, t), F32), jnp.zeros((V_HEAD_DIM, t), F32))
            states = [init, init]
            work = [(j, mp, j == n_tiles - 1) for j in range(n_tiles) for mp in range(2)]
            pending = [score(*w) for w in work[:SCORE_LOOKAHEAD]]
            for n, (j, mp, _) in enumerate(work):
                states[mp] = update(states[mp], pending[n], j)
                if n + SCORE_LOOKAHEAD < len(work):
                    pending.append(score(*work[n + SCORE_LOOKAHEAD]))
            finish(states)
        return run

    for n_tiles in range(1, k_ref.shape[0] + 1):
        pl.when(qi == n_tiles - 1)(q_tile(n_tiles))


def _attn(q, k, vt, lq, gsub, layer, *, batch, seq, lambda_init):
    m, wb = q.shape
    t = ROW_TILE
    nq = seq // t
    heads = wb // V_HEAD_DIM
    k3 = k.reshape(m // t, t, wb)
    kern = functools.partial(_attn_kernel, lambda_init=lambda_init)
    return pl.pallas_call(
        kern,
        out_shape=jax.ShapeDtypeStruct((m, wb), BF16),
        grid=(batch, heads, nq),
        in_specs=[
            pl.BlockSpec((None, 4, DIFF_HEAD_DIM), lambda b, h, i: (layer, 0, 0)),
            pl.BlockSpec((None, V_HEAD_DIM, 1), lambda b, h, i: (layer, 0, 0)),
            pl.BlockSpec((t, V_HEAD_DIM), lambda b, h, i: (b * nq + i, h)),
            pl.BlockSpec((nq, t, V_HEAD_DIM), lambda b, h, i: (b, 0, h)),
            pl.BlockSpec((nq, V_HEAD_DIM, t), lambda b, h, i: (b, h, 0)),
        ],
        out_specs=pl.BlockSpec((t, V_HEAD_DIM), lambda b, h, i: (b * nq + i, h)),
        scratch_shapes=[pltpu.VMEM((t, t), F32), pltpu.VMEM((t, V_HEAD_DIM), BF16),
                        pltpu.VMEM((t, V_HEAD_DIM), BF16)],
        compiler_params=_params(3),
        name="diff_attn",
    )(lq, gsub, q, k3, vt)


def _outproj_kernel(ya_ref, yb_ref, yc_ref, x_ref, w_ref, gn_ref, xo_ref, hn_ref):
    wa = ya_ref.shape[1]
    wb = yb_ref.shape[1]
    for r0 in range(0, x_ref.shape[0], NORM_ROWS):
        rs = slice(r0, r0 + NORM_ROWS)
        acc = x_ref[rs, :] + _dot(ya_ref[rs, :], w_ref[0:wa, :])
        acc = acc + _dot(yb_ref[rs, :], w_ref[wa:wa + wb, :])
        acc = acc + _dot(yc_ref[rs, :], w_ref[wa + wb:, :])
        xo_ref[rs, :] = acc
        hn_ref[rs, :] = _rms(acc, gn_ref[...]).astype(hn_ref.dtype)


def _outproj(ya, yb, yc, x, w, gains, layer):
    m, d = x.shape
    rows = ROW_TILE
    row_block = lambda width: pl.BlockSpec((rows, width), lambda i: (i, 0))
    return pl.pallas_call(
        _outproj_kernel,
        out_shape=[jax.ShapeDtypeStruct((m, d), F32), jax.ShapeDtypeStruct((m, d), BF16)],
        grid=(m // rows,),
        in_specs=[row_block(ya.shape[1]), row_block(yb.shape[1]), row_block(yc.shape[1]), row_block(d),
                  _resident((None,) + w.shape[1:], lambda i: (layer, 0, 0)),
                  pl.BlockSpec((None, 1, d), lambda i: (layer, 0, 0))],
        out_specs=[row_block(d), row_block(d)],
        compiler_params=_params(1),
        name="outproj",
    )(ya, yb, yc, x, w, gains)


def kernel(x, g_ffn1, w_ffn1_gate, w_ffn1_up, w_ffn1_down, g_mix, w_in, g_sga_v, w_sga_s, b_sga_s,
           lambda_qk, g_diff_sub, w_conv, w_out, g_ffn2, w_ffn2_gate, w_ffn2_up, w_ffn2_down, g_final):
    batch, seq, d = x.shape
    depth = w_in.shape[0]
    wa = g_sga_v.shape[-1]
    wb = w_out.shape[1] - 2 * wa
    assert seq % ROW_TILE == 0 and wb % V_HEAD_DIM == 0

    cast = lambda w: w.astype(BF16)
    ffn1_f32 = (w_ffn1_gate, w_ffn1_up, w_ffn1_down)
    ffn2_f32 = (w_ffn2_gate, w_ffn2_up, w_ffn2_down)
    w_ffn = tuple(cast(w[0]) for w in ffn1_f32)
    o = 2 * wa
    wuv = cast(w_in[:, :, 0:o])
    wq = cast(w_in[:, :, o:o + wb])
    wk = cast(w_in[:, :, o + wb:o + 2 * wb])
    wvt = cast(jnp.swapaxes(w_in[:, :, o + 2 * wb:o + 3 * wb], 1, 2))
    wc = cast(w_in[:, :, o + 3 * wb:])
    wo = cast(w_out)
    row = lambda g: g.reshape(g.shape[0], 1, g.shape[-1])
    g1, gm, g2, gv = row(g_ffn1), row(g_mix), row(g_ffn2), row(g_sga_v)
    gf = g_final.reshape(1, 1, d)
    bs_t = jnp.swapaxes(b_sga_s, 1, 2)
    gsub = g_diff_sub.reshape(depth, V_HEAD_DIM, 1)

    xf = x.reshape(batch * seq, d)
    h = _norm(xf, g1, 0)
    for l in range(depth):
        xf, h, *w_ffn = _ffn(h, xf, w_ffn, gm, l, next_weights=ffn2_f32 + (l,))
        ya, q, k, vt, yc = _inproj(h, wuv, wq, wk, wvt, wc, gv, w_sga_s, bs_t, w_conv, l, seq=seq)
        lambda_init = 0.8 - 0.6 * math.exp(-0.3 * l)
        yb = _attn(q, k, vt, lambda_qk, gsub, l, batch=batch, seq=seq, lambda_init=lambda_init)
        xf, h = _outproj(ya, yb, yc, xf, wo, g2, l)
        if l + 1 < depth:
            xf, h, *w_ffn = _ffn(h, xf, w_ffn, g1, l + 1, next_weights=ffn1_f32 + (l + 1,))
        else:
            out = _ffn(h, xf, w_ffn, gf, 0)
    return out.reshape(batch, seq, d)
```

```python
import functools
import math

import jax
import jax.numpy as jnp
from jax import lax
from jax.experimental import pallas as pl
from jax.experimental.pallas import tpu as pltpu

F32 = jnp.float32
BF16 = jnp.bfloat16

EPS = 1e-6
CHUNK = 128
GROUP_A = 128
DIFF_HEAD_DIM = 64
V_HEAD_DIM = 2 * DIFF_HEAD_DIM
CONV_WIDTH = 3
HALO = 8
LOG2E = 1.4426950408889634
NEG_BIG = -1e30

V7X_VMEM_LIMIT_BYTES = 60 * 2**20

FFN_ROWS = 1024
FFN_COLS = 512
FFN_X_SLICES = 8
NORM_ROWS = 256
ROW_TILE = 512
SCORE_LOOKAHEAD = 3


def _params(ndim):
    return pltpu.CompilerParams(dimension_semantics=("arbitrary",) * ndim,
                                vmem_limit_bytes=V7X_VMEM_LIMIT_BYTES)


def _dot(a, b):
    return jnp.dot(a, b, preferred_element_type=F32)


def _rms(xf, g):
    y = xf * lax.rsqrt(jnp.mean(xf * xf, axis=-1, keepdims=True) + EPS)
    return y * g


def _resident(block_shape, index_map):
    return pl.BlockSpec(block_shape, index_map, pipeline_mode=pl.Buffered(1))


def _norm_kernel(x_ref, g_ref, o_ref):
    o_ref[...] = _rms(x_ref[...], g_ref[...]).astype(o_ref.dtype)


def _norm(x, gains, layer):
    m, d = x.shape
    return pl.pallas_call(
        _norm_kernel,
        out_shape=jax.ShapeDtypeStruct((m, d), BF16),
        grid=(m // ROW_TILE,),
        in_specs=[pl.BlockSpec((ROW_TILE, d), lambda i: (i, 0)),
                  pl.BlockSpec((None, 1, d), lambda i: (layer, 0, 0))],
        out_specs=pl.BlockSpec((ROW_TILE, d), lambda i: (i, 0)),
        compiler_params=_params(1),
        name="rmsnorm_in",
    )(x, gains)


def _ffn_kernel(h_ref, xs_ref, wg_ref, wu_ref, wd_ref, gn_ref, *refs, nf, rows_xs, n_cast, final):
    cast_src = refs[0:n_cast]
    if final:
        hn_ref, xo_ref = refs[n_cast:]
    else:
        xo_ref, hn_ref = refs[n_cast:n_cast + 2]
        cast_dst = refs[n_cast + 2:]
        for src_ref, dst_ref in zip(cast_src, cast_dst):
            dst_ref[...] = src_ref[...].astype(dst_ref.dtype)
    f = pl.program_id(1)

    @pl.when(f == 0)
    def _():
        xo_ref[...] = jnp.zeros_like(xo_ref)

    @pl.when(f < FFN_X_SLICES)
    def _():
        r0 = pl.multiple_of(f * rows_xs, rows_xs)
        xo_ref[pl.ds(r0, rows_xs), :] += xs_ref[...]

    h = h_ref[...]
    g = _dot(h, wg_ref[...])
    u = _dot(h, wu_ref[...])
    a = ((0.5 * g) * jax.nn.sigmoid(g) * u).astype(BF16)
    xo_ref[...] += _dot(a, wd_ref[...])

    @pl.when(f == nf - 1)
    def _():
        hn_ref[...] = _rms(xo_ref[...], gn_ref[...]).astype(hn_ref.dtype)


def _ffn(h, x, weights, gains, gain_layer, *, casts=None):
    m, d = x.shape
    wg, wu, wd = weights
    d_ff = wg.shape[-1]
    rows = FFN_ROWS
    nf = d_ff // FFN_COLS
    n_i = m // rows
    final = casts is None
    casts = casts or []
    assert nf >= FFN_X_SLICES and rows % (FFN_X_SLICES * 8) == 0
    rows_xs = rows // FFN_X_SLICES
    kern = functools.partial(_ffn_kernel, nf=nf, rows_xs=rows_xs, n_cast=len(casts), final=final)
    row_block = pl.BlockSpec((rows, d), lambda i, f: (i, 0))
    in_specs = [
        row_block,
        pl.BlockSpec((rows_xs, d),
                     lambda i, f: (i * FFN_X_SLICES + jnp.minimum(f, FFN_X_SLICES - 1), 0)),
        pl.BlockSpec((d, FFN_COLS), lambda i, f: (0, f)),
        pl.BlockSpec((d, FFN_COLS), lambda i, f: (0, f)),
        pl.BlockSpec((FFN_COLS, d), lambda i, f: (f, 0)),
        pl.BlockSpec((None, 1, d), lambda i, f: (gain_layer, 0, 0)),
    ]
    operands = [h, x, wg, wu, wd, gains]
    if final:
        out_shape = jax.ShapeDtypeStruct((m, d), F32)
        out_specs = row_block
        scratch = [pltpu.VMEM((rows, d), F32)]
    else:
        out_shape = [jax.ShapeDtypeStruct((m, d), F32), jax.ShapeDtypeStruct((m, d), BF16)]
        out_specs = [row_block, row_block]
        scratch = []
        for w, layer, by_rows in casts:
            _, r, c = w.shape
            if by_rows:
                br, bc = r // (n_i * nf), c
                index = lambda i, f: (i * nf + f, 0)
            else:
                br, bc = r // n_i, c // nf
                index = lambda i, f: (i, f)
            assert r % br == 0 and c % bc == 0 and (r // br) * (c // bc) == n_i * nf
            assert br % 16 == 0 and (bc % 128 == 0 or bc == c)
            in_specs.append(pl.BlockSpec((None, br, bc),
                                         lambda i, f, layer=layer, index=index: (layer,) + index(i, f)))
            operands.append(w)
            out_shape.append(jax.ShapeDtypeStruct((r, c), BF16))
            out_specs.append(pl.BlockSpec((br, bc), index))
    return pl.pallas_call(
        kern,
        out_shape=out_shape,
        grid=(n_i, nf),
        in_specs=in_specs,
        out_specs=out_specs,
        scratch_shapes=scratch,
        compiler_params=_params(2),
        name="ffn_swiglu_final" if final else "ffn_swiglu",
    )(*operands)


def _inproj_kernel(h_ref, w_ref, gv_ref, ws_ref, bs_ref, wconv_ref,
                   ya_ref, q_ref, k_ref, vt_ref, yc_ref, zs_ref, *, tiles_per_seq, q_scale):
    i = pl.program_id(0)
    rows = h_ref.shape[0]
    wa = ya_ref.shape[1]
    wb = q_ref.shape[1]
    wcw = yc_ref.shape[1]
    h = h_ref[...]
    col = [0]
    for width in (wa, wa, wb, wb, wb, wcw, wcw, wcw):
        col.append(col[-1] + width)
    proj = lambda n: _dot(h, w_ref[:, col[n]:col[n + 1]])

    u = proj(0)
    vn = _rms(proj(1), gv_ref[...]).astype(BF16)
    t_idx = lax.broadcasted_iota(jnp.int32, (CHUNK, CHUNK), 0)
    s_idx = lax.broadcasted_iota(jnp.int32, (CHUNK, CHUNK), 1)
    for g in range(wa // GROUP_A):
        cs = slice(g * GROUP_A, (g + 1) * GROUP_A)
        ws_g = jnp.where(t_idx >= s_idx, ws_ref[g], 0.0).astype(BF16)
        b_g = bs_ref[:, g:g + 1]
        for c in range(rows // CHUNK):
            rs = slice(c * CHUNK, (c + 1) * CHUNK)
            mixed = _dot(ws_g, vn[rs, cs]) + b_g
            ya_ref[rs, cs] = (u[rs, cs] * mixed).astype(ya_ref.dtype)

    q_ref[...] = (proj(2) * q_scale).astype(q_ref.dtype)
    k_ref[...] = proj(3).astype(k_ref.dtype)
    vt_ref[...] = proj(4).T.astype(vt_ref.dtype)

    bg = proj(5)
    z = proj(6) * proj(7)

    @pl.when(i % tiles_per_seq == 0)
    def _():
        zs_ref[0:HALO, :] = jnp.zeros((HALO, wcw), F32)

    zs_ref[HALO:HALO + rows, :] = z
    w = wconv_ref[...]
    y = zs_ref[HALO - 2:HALO - 2 + rows, :] * w[0:1, :]
    y = y + zs_ref[HALO - 1:HALO - 1 + rows, :] * w[1:2, :]
    y = y + z * w[2:3, :]
    yc_ref[...] = (bg * y).astype(yc_ref.dtype)
    zs_ref[0:HALO, :] = zs_ref[rows:rows + HALO, :]


def _inproj(h, w, gv, ws, bs_t, wconv, layer, *, seq, wa, wb):
    m, d = h.shape
    rows = ROW_TILE
    wcw = (w.shape[1] - 2 * wa - 3 * wb) // 3
    kern = functools.partial(_inproj_kernel, tiles_per_seq=seq // rows,
                             q_scale=DIFF_HEAD_DIM ** -0.5 * LOG2E)
    lw = lambda i: (layer, 0, 0)
    return pl.pallas_call(
        kern,
        out_shape=[jax.ShapeDtypeStruct((m, wa), BF16),
                   jax.ShapeDtypeStruct((m, wb), BF16),
                   jax.ShapeDtypeStruct((m, wb), BF16),
                   jax.ShapeDtypeStruct((m // rows, wb, rows), BF16),
                   jax.ShapeDtypeStruct((m, wcw), BF16)],
        grid=(m // rows,),
        in_specs=[
            pl.BlockSpec((rows, d), lambda i: (i, 0)),
            _resident(w.shape, lambda i: (0, 0)),
            pl.BlockSpec((None, 1, wa), lw),
            pl.BlockSpec((None, wa // GROUP_A, CHUNK, CHUNK), lambda i: (layer, 0, 0, 0)),
            pl.BlockSpec((None, CHUNK, wa // GROUP_A), lw),
            pl.BlockSpec((None, CONV_WIDTH, wcw), lw),
        ],
        out_specs=[pl.BlockSpec((rows, wa), lambda i: (i, 0)),
                   pl.BlockSpec((rows, wb), lambda i: (i, 0)),
                   pl.BlockSpec((rows, wb), lambda i: (i, 0)),
                   pl.BlockSpec((None, wb, rows), lambda i: (i, 0, 0)),
                   pl.BlockSpec((rows, wcw), lambda i: (i, 0))],
        scratch_shapes=[pltpu.VMEM((rows + HALO, wcw), F32)],
        compiler_params=_params(1),
        name="inproj_mix_ac",
    )(h, w, gv, ws, bs_t, wconv)


def _split3(x):
    rnd = lambda v: v.astype(BF16).astype(F32)
    hi = rnd(x)
    mid = rnd(x - hi)
    lo = rnd(x - hi - mid)
    return hi, mid, lo


def _attn_kernel(lq_ref, gs_ref, q_ref, k_ref, vt_ref, o_ref, biasd_ref, kc_ref, qc_ref, *, lambda_init):
    hd = pl.program_id(1)
    qi = pl.program_id(2)
    t = q_ref.shape[0]
    slope2 = jnp.exp2(-(jnp.full((1, t), hd + 1, jnp.int32).astype(F32))) * LOG2E

    @pl.when(qi == 0)
    def _():
        kk = lax.broadcasted_iota(jnp.int32, (t, t), 0)
        qq = lax.broadcasted_iota(jnp.int32, (t, t), 1)
        biasd_ref[...] = jnp.where(kk <= qq, (kk - qq).astype(F32) * slope2, NEG_BIG)
        r = lax.broadcasted_iota(jnp.int32, (t, V_HEAD_DIM), 0)
        col = jnp.bitwise_and(lax.broadcasted_iota(jnp.int32, (t, V_HEAD_DIM), 1), DIFF_HEAD_DIM - 1)
        rf = r.astype(F32)
        r_hi = jnp.bitwise_and(r, -16).astype(F32)
        kc = jnp.where(col < 3, r_hi, jnp.where(col < 6, rf - r_hi, jnp.where(col < 9, 1.0, 0.0)))
        kc_ref[...] = kc.astype(BF16)
        s_col = jnp.exp2(-(jnp.full((t, V_HEAD_DIM), hd + 1, jnp.int32).astype(F32))) * LOG2E
        s3 = _split3(s_col)
        w3 = _split3(-(s_col * rf))
        qc = jnp.zeros((t, V_HEAD_DIM), F32)
        for n in range(3):
            qc = jnp.where(col == n, s3[n], qc)
            qc = jnp.where(col == n + 3, s3[n], qc)
            qc = jnp.where(col == n + 6, w3[n], qc)
        qc_ref[...] = qc.astype(BF16)

    q = q_ref[...]
    lane = lax.broadcasted_iota(jnp.int32, q.shape, 1)
    own = (lane < DIFF_HEAD_DIM, lane >= DIFF_HEAD_DIM)
    zero = jnp.zeros_like(q)
    q_plain = [jnp.where(own[mp], q, zero) for mp in range(2)]
    q_bias = [jnp.where(own[mp], q, qc_ref[...]) for mp in range(2)]
    ones_rows = jnp.ones((16, t), BF16)

    def score(j, mp, diag):
        dims = (((1,), (1,)), ((), ()))
        if diag:
            return lax.dot_general(k_ref[j], q_plain[mp], dims, preferred_element_type=F32) + biasd_ref[...]
        k_aug = jnp.where(own[mp], k_ref[j], kc_ref[...])
        return lax.dot_general(k_aug, q_bias[mp], dims, preferred_element_type=F32)

    def update(state, s, j):
        m_old, l_old, a_old = state
        c = slope2 * ((j - qi) * t).astype(F32)
        m_new = jnp.maximum(m_old, jnp.max(s, axis=0, keepdims=True) + c)
        p = jnp.exp2(s - (m_new - c)).astype(BF16)
        alpha = jnp.exp2(m_old - m_new)
        pv = _dot(jnp.concatenate([vt_ref[j], ones_rows], axis=0), p)
        l_new = l_old * alpha + pv[V_HEAD_DIM:V_HEAD_DIM + 1, :]
        a_new = a_old * alpha + pv[0:V_HEAD_DIM, :]
        return m_new, l_new, a_new

    def finish(states):
        (_, l1, a1), (_, l2, a2) = states
        lq = lq_ref[...]
        s01 = jnp.sum(lq[0:1, :] * lq[1:2, :], axis=1, keepdims=True)
        s23 = jnp.sum(lq[2:3, :] * lq[3:4, :], axis=1, keepdims=True)
        lam = jnp.exp(s01) - jnp.exp(s23) + lambda_init
        o = a1 * (1.0 / l1) - lam * (a2 * (1.0 / l2))
        y = o * lax.rsqrt(jnp.mean(o * o, axis=0, keepdims=True) + EPS)
        y = y * gs_ref[...] * (1.0 - lambda_init)
        o_ref[...] = y.T.astype(o_ref.dtype)

    def q_tile(n_tiles):
        def run():
            init = (jnp.full((1, t), NEG_BIG, F32), jnp.zeros((1, t), F32), jnp.zeros((V_HEAD_DIM, t), F32))
            states = [init, init]
            work = [(j, mp, j == n_tiles - 1) for j in range(n_tiles) for mp in range(2)]
            pending = [score(*w) for w in work[:SCORE_LOOKAHEAD]]
            for n, (j, mp, _) in enumerate(work):
                states[mp] = update(states[mp], pending[n], j)
                if n + SCORE_LOOKAHEAD < len(work):
                    pending.append(score(*work[n + SCORE_LOOKAHEAD]))
            finish(states)
        return run

    for n_tiles in range(1, k_ref.shape[0] + 1):
        pl.when(qi == n_tiles - 1)(q_tile(n_tiles))


def _attn(q, k, vt, lq, gsub, layer, *, batch, seq, lambda_init):
    m, wb = q.shape
    t = ROW_TILE
    nq = seq // t
    heads = wb // V_HEAD_DIM
    k3 = k.reshape(m // t, t, wb)
    kern = functools.partial(_attn_kernel, lambda_init=lambda_init)
    return pl.pallas_call(
        kern,
        out_shape=jax.ShapeDtypeStruct((m, wb), BF16),
        grid=(batch, heads, nq),
        in_specs=[
            pl.BlockSpec((None, 4, DIFF_HEAD_DIM), lambda b, h, i: (layer, 0, 0)),
            pl.BlockSpec((None, V_HEAD_DIM, 1), lambda b, h, i: (layer, 0, 0)),
            pl.BlockSpec((t, V_HEAD_DIM), lambda b, h, i: (b * nq + i, h)),
            pl.BlockSpec((nq, t, V_HEAD_DIM), lambda b, h, i: (b, 0, h)),
            pl.BlockSpec((nq, V_HEAD_DIM, t), lambda b, h, i: (b, h, 0)),
        ],
        out_specs=pl.BlockSpec((t, V_HEAD_DIM), lambda b, h, i: (b * nq + i, h)),
        scratch_shapes=[pltpu.VMEM((t, t), F32), pltpu.VMEM((t, V_HEAD_DIM), BF16),
                        pltpu.VMEM((t, V_HEAD_DIM), BF16)],
        compiler_params=_params(3),
        name="diff_attn",
    )(lq, gsub, q, k3, vt)


def _outproj_kernel(ya_ref, yb_ref, yc_ref, x_ref, w_ref, gn_ref, xo_ref, hn_ref):
    wa = ya_ref.shape[1]
    wb = yb_ref.shape[1]
    for r0 in range(0, x_ref.shape[0], NORM_ROWS):
        rs = slice(r0, r0 + NORM_ROWS)
        acc = x_ref[rs, :] + _dot(ya_ref[rs, :], w_ref[0:wa, :])
        acc = acc + _dot(yb_ref[rs, :], w_ref[wa:wa + wb, :])
        acc = acc + _dot(yc_ref[rs, :], w_ref[wa + wb:, :])
        xo_ref[rs, :] = acc
        hn_ref[rs, :] = _rms(acc, gn_ref[...]).astype(hn_ref.dtype)


def _outproj(ya, yb, yc, x, w, gains, layer):
    m, d = x.shape
    rows = ROW_TILE
    row_block = lambda width: pl.BlockSpec((rows, width), lambda i: (i, 0))
    return pl.pallas_call(
        _outproj_kernel,
        out_shape=[jax.ShapeDtypeStruct((m, d), F32), jax.ShapeDtypeStruct((m, d), BF16)],
        grid=(m // rows,),
        in_specs=[row_block(ya.shape[1]), row_block(yb.shape[1]), row_block(yc.shape[1]), row_block(d),
                  _resident((None,) + w.shape[1:], lambda i: (layer, 0, 0)),
                  pl.BlockSpec((None, 1, d), lambda i: (layer, 0, 0))],
        out_specs=[row_block(d), row_block(d)],
        compiler_params=_params(1),
        name="outproj",
    )(ya, yb, yc, x, w, gains)


def kernel(x, g_ffn1, w_ffn1_gate, w_ffn1_up, w_ffn1_down, g_mix, w_in, g_sga_v, w_sga_s, b_sga_s,
           lambda_qk, g_diff_sub, w_conv, w_out, g_ffn2, w_ffn2_gate, w_ffn2_up, w_ffn2_down, g_final):
    batch, seq, d = x.shape
    depth = w_in.shape[0]
    wa = g_sga_v.shape[-1]
    wb = w_out.shape[1] - 2 * wa
    assert seq % ROW_TILE == 0 and wb % V_HEAD_DIM == 0

    cast = lambda w: w.astype(BF16)
    ffn1_f32 = (w_ffn1_gate, w_ffn1_up, w_ffn1_down)
    ffn2_f32 = (w_ffn2_gate, w_ffn2_up, w_ffn2_down)
    ffn_casts = lambda ws, layer: [(ws[0], layer, False), (ws[1], layer, False), (ws[2], layer, True)]
    w_ffn = tuple(cast(w[0]) for w in ffn1_f32)
    w_in_l = cast(w_in[0])
    wo = cast(w_out)
    row = lambda g: g.reshape(g.shape[0], 1, g.shape[-1])
    g1, gm, g2, gv = row(g_ffn1), row(g_mix), row(g_ffn2), row(g_sga_v)
    gf = g_final.reshape(1, 1, d)
    bs_t = jnp.swapaxes(b_sga_s, 1, 2)
    gsub = g_diff_sub.reshape(depth, V_HEAD_DIM, 1)

    xf = x.reshape(batch * seq, d)
    h = _norm(xf, g1, 0)
    for l in range(depth):
        xf, h, *w_ffn = _ffn(h, xf, w_ffn, gm, l, casts=ffn_casts(ffn2_f32, l))
        ya, q, k, vt, yc = _inproj(h, w_in_l, gv, w_sga_s, bs_t, w_conv, l, seq=seq, wa=wa, wb=wb)
        lambda_init = 0.8 - 0.6 * math.exp(-0.3 * l)
        yb = _attn(q, k, vt, lambda_qk, gsub, l, batch=batch, seq=seq, lambda_init=lambda_init)
        xf, h = _outproj(ya, yb, yc, xf, wo, g2, l)
        if l + 1 < depth:
            xf, h, *w_next = _ffn(h, xf, w_ffn, g1, l + 1,
                                  casts=ffn_casts(ffn1_f32, l + 1) + [(w_in, l + 1, False)])
            w_ffn, w_in_l = w_next[0:3], w_next[3]
        else:
            out = _ffn(h, xf, w_ffn, gf, 0)
    return out.reshape(batch, seq, d)
```

```python
import functools
import math

import jax
import jax.numpy as jnp
from jax import lax
from jax.experimental import pallas as pl
from jax.experimental.pallas import tpu as pltpu

F32 = jnp.float32
BF16 = jnp.bfloat16

EPS = 1e-6
CHUNK = 128
GROUP_A = 128
DIFF_HEAD_DIM = 64
V_HEAD_DIM = 2 * DIFF_HEAD_DIM
CONV_WIDTH = 3
HALO = 8
LOG2E = 1.4426950408889634
NEG_BIG = -1e30

V7X_VMEM_LIMIT_BYTES = 60 * 2**20

FFN_ROWS = 1024
FFN_COLS = 512
FFN_X_SLICES = 8
NORM_ROWS = 256
ROW_TILE = 512
SCORE_LOOKAHEAD = 3


def _params(ndim):
    return pltpu.CompilerParams(dimension_semantics=("arbitrary",) * ndim,
                                vmem_limit_bytes=V7X_VMEM_LIMIT_BYTES)


def _dot(a, b):
    return jnp.dot(a, b, preferred_element_type=F32)


def _rms(xf, g):
    y = xf * lax.rsqrt(jnp.mean(xf * xf, axis=-1, keepdims=True) + EPS)
    return y * g


def _resident(block_shape, index_map):
    return pl.BlockSpec(block_shape, index_map, pipeline_mode=pl.Buffered(1))


def _norm_kernel(x_ref, g_ref, o_ref):
    o_ref[...] = _rms(x_ref[...], g_ref[...]).astype(o_ref.dtype)


def _norm(x, gains, layer):
    m, d = x.shape
    return pl.pallas_call(
        _norm_kernel,
        out_shape=jax.ShapeDtypeStruct((m, d), BF16),
        grid=(m // ROW_TILE,),
        in_specs=[pl.BlockSpec((ROW_TILE, d), lambda i: (i, 0)),
                  pl.BlockSpec((None, 1, d), lambda i: (layer, 0, 0))],
        out_specs=pl.BlockSpec((ROW_TILE, d), lambda i: (i, 0)),
        compiler_params=_params(1),
        name="rmsnorm_in",
    )(x, gains)


def _ffn_kernel(h_ref, xs_ref, wg_ref, wu_ref, wd_ref, gn_ref, *refs, nf, rows_xs, n_cast, final):
    cast_src = refs[0:n_cast]
    if final:
        hn_ref, xo_ref = refs[n_cast:]
    else:
        xo_ref, hn_ref = refs[n_cast:n_cast + 2]
        cast_dst = refs[n_cast + 2:]
        for src_ref, dst_ref in zip(cast_src, cast_dst):
            dst_ref[...] = src_ref[...].astype(dst_ref.dtype)
    f = pl.program_id(1)

    @pl.when(f == 0)
    def _():
        xo_ref[...] = jnp.zeros_like(xo_ref)

    @pl.when(f < FFN_X_SLICES)
    def _():
        r0 = pl.multiple_of(f * rows_xs, rows_xs)
        xo_ref[pl.ds(r0, rows_xs), :] += xs_ref[...]

    h = h_ref[...]
    g = _dot(h, wg_ref[...])
    u = _dot(h, wu_ref[...])
    a = ((0.5 * g) * jax.nn.sigmoid(g) * u).astype(BF16)
    xo_ref[...] += _dot(a, wd_ref[...])

    @pl.when(f == nf - 1)
    def _():
        hn_ref[...] = _rms(xo_ref[...], gn_ref[...]).astype(hn_ref.dtype)


def _ffn(h, x, weights, gains, gain_layer, *, casts=None):
    m, d = x.shape
    wg, wu, wd = weights
    d_ff = wg.shape[-1]
    rows = FFN_ROWS
    nf = d_ff // FFN_COLS
    n_i = m // rows
    final = casts is None
    casts = casts or []
    assert nf >= FFN_X_SLICES and rows % (FFN_X_SLICES * 8) == 0
    rows_xs = rows // FFN_X_SLICES
    kern = functools.partial(_ffn_kernel, nf=nf, rows_xs=rows_xs, n_cast=len(casts), final=final)
    row_block = pl.BlockSpec((rows, d), lambda i, f: (i, 0))
    in_specs = [
        row_block,
        pl.BlockSpec((rows_xs, d),
                     lambda i, f: (i * FFN_X_SLICES + jnp.minimum(f, FFN_X_SLICES - 1), 0)),
        pl.BlockSpec((d, FFN_COLS), lambda i, f: (0, f)),
        pl.BlockSpec((d, FFN_COLS), lambda i, f: (0, f)),
        pl.BlockSpec((FFN_COLS, d), lambda i, f: (f, 0)),
        pl.BlockSpec((None, 1, d), lambda i, f: (gain_layer, 0, 0)),
    ]
    operands = [h, x, wg, wu, wd, gains]
    if final:
        out_shape = jax.ShapeDtypeStruct((m, d), F32)
        out_specs = row_block
        scratch = [pltpu.VMEM((rows, d), F32)]
    else:
        out_shape = [jax.ShapeDtypeStruct((m, d), F32), jax.ShapeDtypeStruct((m, d), BF16)]
        out_specs = [row_block, row_block]
        scratch = []
        for w, layer, by_rows in casts:
            _, r, c = w.shape
            if by_rows:
                br, bc = r // (n_i * nf), c
                index = lambda i, f: (i * nf + f, 0)
            else:
                br, bc = r // n_i, c // nf
                index = lambda i, f: (i, f)
            assert r % br == 0 and c % bc == 0 and (r // br) * (c // bc) == n_i * nf
            assert br % 16 == 0 and (bc % 128 == 0 or bc == c)
            in_specs.append(pl.BlockSpec((None, br, bc),
                                         lambda i, f, layer=layer, index=index: (layer,) + index(i, f)))
            operands.append(w)
            out_shape.append(jax.ShapeDtypeStruct((r, c), BF16))
            out_specs.append(pl.BlockSpec((br, bc), index))
    return pl.pallas_call(
        kern,
        out_shape=out_shape,
        grid=(n_i, nf),
        in_specs=in_specs,
        out_specs=out_specs,
        scratch_shapes=scratch,
        compiler_params=_params(2),
        name="ffn_swiglu_final" if final else "ffn_swiglu",
    )(*operands)


def _inproj_kernel(h_ref, w_ref, gv_ref, ws_ref, bs_ref, wconv_ref, wo_f32_ref,
                   ya_ref, q_ref, k_ref, vt_ref, yc_ref, wo_ref, zs_ref, *, tiles_per_seq, q_scale):
    i = pl.program_id(0)
    wo_ref[...] = wo_f32_ref[...].astype(wo_ref.dtype)
    rows = h_ref.shape[0]
    wa = ya_ref.shape[1]
    wb = q_ref.shape[1]
    wcw = yc_ref.shape[1]
    h = h_ref[...]
    col = [0]
    for width in (wa, wa, wb, wb, wb, wcw, wcw, wcw):
        col.append(col[-1] + width)
    proj = lambda n: _dot(h, w_ref[:, col[n]:col[n + 1]])

    u = proj(0)
    vn = _rms(proj(1), gv_ref[...]).astype(BF16)
    t_idx = lax.broadcasted_iota(jnp.int32, (CHUNK, CHUNK), 0)
    s_idx = lax.broadcasted_iota(jnp.int32, (CHUNK, CHUNK), 1)
    for g in range(wa // GROUP_A):
        cs = slice(g * GROUP_A, (g + 1) * GROUP_A)
        ws_g = jnp.where(t_idx >= s_idx, ws_ref[g], 0.0).astype(BF16)
        b_g = bs_ref[:, g:g + 1]
        for c in range(rows // CHUNK):
            rs = slice(c * CHUNK, (c + 1) * CHUNK)
            mixed = _dot(ws_g, vn[rs, cs]) + b_g
            ya_ref[rs, cs] = (u[rs, cs] * mixed).astype(ya_ref.dtype)

    q_ref[...] = (proj(2) * q_scale).astype(q_ref.dtype)
    k_ref[...] = proj(3).astype(k_ref.dtype)
    vt_ref[...] = proj(4).T.astype(vt_ref.dtype)

    bg = proj(5)
    z = proj(6) * proj(7)

    @pl.when(i % tiles_per_seq == 0)
    def _():
        zs_ref[0:HALO, :] = jnp.zeros((HALO, wcw), F32)

    zs_ref[HALO:HALO + rows, :] = z
    w = wconv_ref[...]
    y = zs_ref[HALO - 2:HALO - 2 + rows, :] * w[0:1, :]
    y = y + zs_ref[HALO - 1:HALO - 1 + rows, :] * w[1:2, :]
    y = y + z * w[2:3, :]
    yc_ref[...] = (bg * y).astype(yc_ref.dtype)
    zs_ref[0:HALO, :] = zs_ref[rows:rows + HALO, :]


def _inproj(h, w, gv, ws, bs_t, wconv, w_out, layer, *, seq, wa, wb):
    m, d = h.shape
    rows = ROW_TILE
    wcw = (w.shape[1] - 2 * wa - 3 * wb) // 3
    mix = w_out.shape[1]
    wo_rows = mix // (m // rows)
    assert wo_rows * (m // rows) == mix and wo_rows % 16 == 0
    kern = functools.partial(_inproj_kernel, tiles_per_seq=seq // rows,
                             q_scale=DIFF_HEAD_DIM ** -0.5 * LOG2E)
    lw = lambda i: (layer, 0, 0)
    return pl.pallas_call(
        kern,
        out_shape=[jax.ShapeDtypeStruct((m, wa), BF16),
                   jax.ShapeDtypeStruct((m, wb), BF16),
                   jax.ShapeDtypeStruct((m, wb), BF16),
                   jax.ShapeDtypeStruct((m // rows, wb, rows), BF16),
                   jax.ShapeDtypeStruct((m, wcw), BF16),
                   jax.ShapeDtypeStruct((mix, d), BF16)],
        grid=(m // rows,),
        in_specs=[
            pl.BlockSpec((rows, d), lambda i: (i, 0)),
            _resident(w.shape, lambda i: (0, 0)),
            pl.BlockSpec((None, 1, wa), lw),
            pl.BlockSpec((None, wa // GROUP_A, CHUNK, CHUNK), lambda i: (layer, 0, 0, 0)),
            pl.BlockSpec((None, CHUNK, wa // GROUP_A), lw),
            pl.BlockSpec((None, CONV_WIDTH, wcw), lw),
            pl.BlockSpec((None, wo_rows, d), lambda i: (layer, i, 0)),
        ],
        out_specs=[pl.BlockSpec((rows, wa), lambda i: (i, 0)),
                   pl.BlockSpec((rows, wb), lambda i: (i, 0)),
                   pl.BlockSpec((rows, wb), lambda i: (i, 0)),
                   pl.BlockSpec((None, wb, rows), lambda i: (i, 0, 0)),
                   pl.BlockSpec((rows, wcw), lambda i: (i, 0)),
                   pl.BlockSpec((wo_rows, d), lambda i: (i, 0))],
        scratch_shapes=[pltpu.VMEM((rows + HALO, wcw), F32)],
        compiler_params=_params(1),
        name="inproj_mix_ac",
    )(h, w, gv, ws, bs_t, wconv, w_out)


def _split3(x):
    rnd = lambda v: v.astype(BF16).astype(F32)
    hi = rnd(x)
    mid = rnd(x - hi)
    lo = rnd(x - hi - mid)
    return hi, mid, lo


def _attn_kernel(lq_ref, gs_ref, q_ref, k_ref, vt_ref, o_ref, biasd_ref, kc_ref, qc_ref, *, lambda_init):
    hd = pl.program_id(1)
    qi = pl.program_id(2)
    t = q_ref.shape[0]
    slope2 = jnp.exp2(-(jnp.full((1, t), hd + 1, jnp.int32).astype(F32))) * LOG2E

    @pl.when(qi == 0)
    def _():
        kk = lax.broadcasted_iota(jnp.int32, (t, t), 0)
        qq = lax.broadcasted_iota(jnp.int32, (t, t), 1)
        biasd_ref[...] = jnp.where(kk <= qq, (kk - qq).astype(F32) * slope2, NEG_BIG)
        r = lax.broadcasted_iota(jnp.int32, (t, V_HEAD_DIM), 0)
        col = jnp.bitwise_and(lax.broadcasted_iota(jnp.int32, (t, V_HEAD_DIM), 1), DIFF_HEAD_DIM - 1)
        rf = r.astype(F32)
        r_hi = jnp.bitwise_and(r, -16).astype(F32)
        kc = jnp.where(col < 3, r_hi, jnp.where(col < 6, rf - r_hi, jnp.where(col < 9, 1.0, 0.0)))
        kc_ref[...] = kc.astype(BF16)
        s_col = jnp.exp2(-(jnp.full((t, V_HEAD_DIM), hd + 1, jnp.int32).astype(F32))) * LOG2E
        s3 = _split3(s_col)
        w3 = _split3(-(s_col * rf))
        qc = jnp.zeros((t, V_HEAD_DIM), F32)
        for n in range(3):
            qc = jnp.where(col == n, s3[n], qc)
            qc = jnp.where(col == n + 3, s3[n], qc)
            qc = jnp.where(col == n + 6, w3[n], qc)
        qc_ref[...] = qc.astype(BF16)

    q = q_ref[...]
    lane = lax.broadcasted_iota(jnp.int32, q.shape, 1)
    own = (lane < DIFF_HEAD_DIM, lane >= DIFF_HEAD_DIM)
    zero = jnp.zeros_like(q)
    q_plain = [jnp.where(own[mp], q, zero) for mp in range(2)]
    q_bias = [jnp.where(own[mp], q, qc_ref[...]) for mp in range(2)]
    ones_rows = jnp.ones((16, t), BF16)

    half = t // 2

    def score(j, mp, diag):
        dims = (((1,), (1,)), ((), ()))
        if diag:
            k = k_ref[j]
            first = lax.dot_general(k[0:half, :], q_plain[mp], dims, preferred_element_type=F32)
            second = lax.dot_general(k[half:, :], q_plain[mp][half:, :], dims, preferred_element_type=F32)
            masked = jnp.full((half, half), NEG_BIG, F32)
            s = jnp.concatenate([first, jnp.concatenate([masked, second], axis=1)], axis=0)
            return s + biasd_ref[...]
        k_aug = jnp.where(own[mp], k_ref[j], kc_ref[...])
        return lax.dot_general(k_aug, q_bias[mp], dims, preferred_element_type=F32)

    def update(state, s, j, diag):
        c = slope2 * ((j - qi) * t).astype(F32)
        m_new = jnp.max(s, axis=0, keepdims=True) + c
        if state is not None:
            m_old, l_old, a_old = state
            m_new = jnp.maximum(m_old, m_new)
        p = jnp.exp2(s - (m_new - c)).astype(BF16)
        vt = jnp.concatenate([vt_ref[j], ones_rows], axis=0)
        if diag:
            pv_second = _dot(vt[:, half:], p[half:, half:])
            pv = _dot(vt[:, 0:half], p[0:half, :]) + jnp.concatenate([jnp.zeros_like(pv_second), pv_second], axis=1)
        else:
            pv = _dot(vt, p)
        l_new, a_new = pv[V_HEAD_DIM:V_HEAD_DIM + 1, :], pv[0:V_HEAD_DIM, :]
        if state is not None:
            alpha = jnp.exp2(m_old - m_new)
            l_new, a_new = l_old * alpha + l_new, a_old * alpha + a_new
        return m_new, l_new, a_new

    def finish(states):
        (_, l1, a1), (_, l2, a2) = states
        lq = lq_ref[...]
        s01 = jnp.sum(lq[0:1, :] * lq[1:2, :], axis=1, keepdims=True)
        s23 = jnp.sum(lq[2:3, :] * lq[3:4, :], axis=1, keepdims=True)
        lam = jnp.exp(s01) - jnp.exp(s23) + lambda_init
        o = a1 * (1.0 / l1) - lam * (a2 * (1.0 / l2))
        y = o * lax.rsqrt(jnp.mean(o * o, axis=0, keepdims=True) + EPS)
        y = y * gs_ref[...] * (1.0 - lambda_init)
        o_ref[...] = y.T.astype(o_ref.dtype)

    def q_tile(n_tiles):
        def run():
            states = [None, None]
            work = [(j, mp, j == n_tiles - 1) for j in range(n_tiles) for mp in range(2)]
            pending = [score(*w) for w in work[:SCORE_LOOKAHEAD]]
            for n, (j, mp, diag) in enumerate(work):
                states[mp] = update(states[mp], pending[n], j, diag)
                if n + SCORE_LOOKAHEAD < len(work):
                    pending.append(score(*work[n + SCORE_LOOKAHEAD]))
            finish(states)
        return run

    for n_tiles in range(1, k_ref.shape[0] + 1):
        pl.when(qi == n_tiles - 1)(q_tile(n_tiles))


def _attn(q, k, vt, lq, gsub, layer, *, batch, seq, lambda_init):
    m, wb = q.shape
    t = ROW_TILE
    nq = seq // t
    heads = wb // V_HEAD_DIM
    k3 = k.reshape(m // t, t, wb)
    kern = functools.partial(_attn_kernel, lambda_init=lambda_init)
    return pl.pallas_call(
        kern,
        out_shape=jax.ShapeDtypeStruct((m, wb), BF16),
        grid=(batch, heads, nq),
        in_specs=[
            pl.BlockSpec((None, 4, DIFF_HEAD_DIM), lambda b, h, i: (layer, 0, 0)),
            pl.BlockSpec((None, V_HEAD_DIM, 1), lambda b, h, i: (layer, 0, 0)),
            pl.BlockSpec((t, V_HEAD_DIM), lambda b, h, i: (b * nq + i, h)),
            pl.BlockSpec((nq, t, V_HEAD_DIM), lambda b, h, i: (b, 0, h)),
            pl.BlockSpec((nq, V_HEAD_DIM, t), lambda b, h, i: (b, h, 0)),
        ],
        out_specs=pl.BlockSpec((t, V_HEAD_DIM), lambda b, h, i: (b * nq + i, h)),
        scratch_shapes=[pltpu.VMEM((t, t), F32), pltpu.VMEM((t, V_HEAD_DIM), BF16),
                        pltpu.VMEM((t, V_HEAD_DIM), BF16)],
        compiler_params=_params(3),
        name="diff_attn",
    )(lq, gsub, q, k3, vt)


def _outproj_kernel(ya_ref, yb_ref, yc_ref, x_ref, w_ref, gn_ref, xo_ref, hn_ref):
    wa = ya_ref.shape[1]
    wb = yb_ref.shape[1]
    for r0 in range(0, x_ref.shape[0], NORM_ROWS):
        rs = slice(r0, r0 + NORM_ROWS)
        acc = x_ref[rs, :] + _dot(ya_ref[rs, :], w_ref[0:wa, :])
        acc = acc + _dot(yb_ref[rs, :], w_ref[wa:wa + wb, :])
        acc = acc + _dot(yc_ref[rs, :], w_ref[wa + wb:, :])
        xo_ref[rs, :] = acc
        hn_ref[rs, :] = _rms(acc, gn_ref[...]).astype(hn_ref.dtype)


def _outproj(ya, yb, yc, x, w, gains, layer):
    m, d = x.shape
    rows = ROW_TILE
    row_block = lambda width: pl.BlockSpec((rows, width), lambda i: (i, 0))
    return pl.pallas_call(
        _outproj_kernel,
        out_shape=[jax.ShapeDtypeStruct((m, d), F32), jax.ShapeDtypeStruct((m, d), BF16)],
        grid=(m // rows,),
        in_specs=[row_block(ya.shape[1]), row_block(yb.shape[1]), row_block(yc.shape[1]), row_block(d),
                  _resident(w.shape, lambda i: (0, 0)),
                  pl.BlockSpec((None, 1, d), lambda i: (layer, 0, 0))],
        out_specs=[row_block(d), row_block(d)],
        compiler_params=_params(1),
        name="outproj",
    )(ya, yb, yc, x, w, gains)


def kernel(x, g_ffn1, w_ffn1_gate, w_ffn1_up, w_ffn1_down, g_mix, w_in, g_sga_v, w_sga_s, b_sga_s,
           lambda_qk, g_diff_sub, w_conv, w_out, g_ffn2, w_ffn2_gate, w_ffn2_up, w_ffn2_down, g_final):
    batch, seq, d = x.shape
    depth = w_in.shape[0]
    wa = g_sga_v.shape[-1]
    wb = w_out.shape[1] - 2 * wa
    assert seq % ROW_TILE == 0 and wb % V_HEAD_DIM == 0

    cast = lambda w: w.astype(BF16)
    ffn1_f32 = (w_ffn1_gate, w_ffn1_up, w_ffn1_down)
    ffn2_f32 = (w_ffn2_gate, w_ffn2_up, w_ffn2_down)
    ffn_casts = lambda ws, layer: [(ws[0], layer, False), (ws[1], layer, False), (ws[2], layer, True)]
    w_ffn = tuple(cast(w[0]) for w in ffn1_f32)
    w_in_l = cast(w_in[0])
    row = lambda g: g.reshape(g.shape[0], 1, g.shape[-1])
    g1, gm, g2, gv = row(g_ffn1), row(g_mix), row(g_ffn2), row(g_sga_v)
    gf = g_final.reshape(1, 1, d)
    bs_t = jnp.swapaxes(b_sga_s, 1, 2)
    gsub = g_diff_sub.reshape(depth, V_HEAD_DIM, 1)

    xf = x.reshape(batch * seq, d)
    h = _norm(xf, g1, 0)
    for l in range(depth):
        xf, h, *w_ffn = _ffn(h, xf, w_ffn, gm, l, casts=ffn_casts(ffn2_f32, l))
        ya, q, k, vt, yc, wo = _inproj(h, w_in_l, gv, w_sga_s, bs_t, w_conv, w_out, l, seq=seq, wa=wa, wb=wb)
        lambda_init = 0.8 - 0.6 * math.exp(-0.3 * l)
        yb = _attn(q, k, vt, lambda_qk, gsub, l, batch=batch, seq=seq, lambda_init=lambda_init)
        xf, h = _outproj(ya, yb, yc, xf, wo, g2, l)
        if l + 1 < depth:
            xf, h, *w_next = _ffn(h, xf, w_ffn, g1, l + 1,
                                  casts=ffn_casts(ffn1_f32, l + 1) + [(w_in, l + 1, False)])
            w_ffn, w_in_l = w_next[0:3], w_next[3]
        else:
            out = _ffn(h, xf, w_ffn, gf, 0)
    return out.reshape(batch, seq, d)
```

```python
import functools
import math

import jax
import jax.numpy as jnp
from jax import lax
from jax.experimental import pallas as pl
from jax.experimental.pallas import tpu as pltpu

F32 = jnp.float32
BF16 = jnp.bfloat16

EPS = 1e-6
CHUNK = 128
GROUP_A = 128
DIFF_HEAD_DIM = 64
V_HEAD_DIM = 2 * DIFF_HEAD_DIM
CONV_WIDTH = 3
HALO = 8
LOG2E = 1.4426950408889634
NEG_BIG = -1e30

V7X_VMEM_LIMIT_BYTES = 60 * 2**20

FFN_ROWS = 1024
FFN_COLS = 512
FFN_X_SLICES = 8
NORM_ROWS = 256
ROW_TILE = 512
SCORE_LOOKAHEAD = 3


def _params(ndim):
    return pltpu.CompilerParams(dimension_semantics=("arbitrary",) * ndim,
                                vmem_limit_bytes=V7X_VMEM_LIMIT_BYTES)


def _dot(a, b):
    return jnp.dot(a, b, preferred_element_type=F32)


def _rms(xf, g):
    y = xf * lax.rsqrt(jnp.mean(xf * xf, axis=-1, keepdims=True) + EPS)
    return y * g


def _resident(block_shape, index_map):
    return pl.BlockSpec(block_shape, index_map, pipeline_mode=pl.Buffered(1))


def _norm_kernel(x_ref, g_ref, o_ref):
    o_ref[...] = _rms(x_ref[...], g_ref[...]).astype(o_ref.dtype)


def _norm(x, gains, layer):
    m, d = x.shape
    return pl.pallas_call(
        _norm_kernel,
        out_shape=jax.ShapeDtypeStruct((m, d), BF16),
        grid=(m // ROW_TILE,),
        in_specs=[pl.BlockSpec((ROW_TILE, d), lambda i: (i, 0)),
                  pl.BlockSpec((None, 1, d), lambda i: (layer, 0, 0))],
        out_specs=pl.BlockSpec((ROW_TILE, d), lambda i: (i, 0)),
        compiler_params=_params(1),
        name="rmsnorm_in",
    )(x, gains)


def _ffn_kernel(h_ref, xs_ref, wg_ref, wu_ref, wd_ref, gn_ref, *refs, nf, rows_xs, n_cast, final):
    cast_src = refs[0:n_cast]
    if final:
        hn_ref, xo_ref = refs[n_cast:]
    else:
        xo_ref, hn_ref = refs[n_cast:n_cast + 2]
        cast_dst = refs[n_cast + 2:]
        for src_ref, dst_ref in zip(cast_src, cast_dst):
            dst_ref[...] = src_ref[...].astype(dst_ref.dtype)
    f = pl.program_id(1)

    @pl.when(f == 0)
    def _():
        xo_ref[...] = jnp.zeros_like(xo_ref)

    @pl.when(f < FFN_X_SLICES)
    def _():
        r0 = pl.multiple_of(f * rows_xs, rows_xs)
        xo_ref[pl.ds(r0, rows_xs), :] += xs_ref[...]

    h = h_ref[...]
    g = _dot(h, wg_ref[...])
    u = _dot(h, wu_ref[...])
    a = ((0.5 * g) * jax.nn.sigmoid(g) * u).astype(BF16)
    xo_ref[...] += _dot(a, wd_ref[...])

    @pl.when(f == nf - 1)
    def _():
        hn_ref[...] = _rms(xo_ref[...], gn_ref[...]).astype(hn_ref.dtype)


def _ffn(h, x, weights, gains, gain_layer, *, casts=None):
    m, d = x.shape
    wg, wu, wd = weights
    d_ff = wg.shape[-1]
    rows = FFN_ROWS
    nf = d_ff // FFN_COLS
    n_i = m // rows
    final = casts is None
    casts = casts or []
    assert nf >= FFN_X_SLICES and rows % (FFN_X_SLICES * 8) == 0
    rows_xs = rows // FFN_X_SLICES
    kern = functools.partial(_ffn_kernel, nf=nf, rows_xs=rows_xs, n_cast=len(casts), final=final)
    row_block = pl.BlockSpec((rows, d), lambda i, f: (i, 0))
    in_specs = [
        row_block,
        pl.BlockSpec((rows_xs, d),
                     lambda i, f: (i * FFN_X_SLICES + jnp.minimum(f, FFN_X_SLICES - 1), 0)),
        pl.BlockSpec((d, FFN_COLS), lambda i, f: (0, f)),
        pl.BlockSpec((d, FFN_COLS), lambda i, f: (0, f)),
        pl.BlockSpec((FFN_COLS, d), lambda i, f: (f, 0)),
        pl.BlockSpec((None, 1, d), lambda i, f: (gain_layer, 0, 0)),
    ]
    operands = [h, x, wg, wu, wd, gains]
    if final:
        out_shape = jax.ShapeDtypeStruct((m, d), F32)
        out_specs = row_block
        scratch = [pltpu.VMEM((rows, d), F32)]
    else:
        out_shape = [jax.ShapeDtypeStruct((m, d), F32), jax.ShapeDtypeStruct((m, d), BF16)]
        out_specs = [row_block, row_block]
        scratch = []
        for w, layer, by_rows in casts:
            _, r, c = w.shape
            if by_rows:
                br, bc = r // (n_i * nf), c
                index = lambda i, f: (i * nf + f, 0)
            else:
                br, bc = r // n_i, c // nf
                index = lambda i, f: (i, f)
            assert r % br == 0 and c % bc == 0 and (r // br) * (c // bc) == n_i * nf
            assert br % 16 == 0 and (bc % 128 == 0 or bc == c)
            in_specs.append(pl.BlockSpec((None, br, bc),
                                         lambda i, f, layer=layer, index=index: (layer,) + index(i, f)))
            operands.append(w)
            out_shape.append(jax.ShapeDtypeStruct((r, c), BF16))
            out_specs.append(pl.BlockSpec((br, bc), index))
    return pl.pallas_call(
        kern,
        out_shape=out_shape,
        grid=(n_i, nf),
        in_specs=in_specs,
        out_specs=out_specs,
        scratch_shapes=scratch,
        compiler_params=_params(2),
        name="ffn_swiglu_final" if final else "ffn_swiglu",
    )(*operands)


def _inproj_kernel(h_ref, w_ref, gv_ref, ws_ref, bs_ref, wconv_ref, wo_f32_ref,
                   ya_ref, q_ref, k_ref, vt_ref, yc_ref, wo_ref, zs_ref, *, tiles_per_seq, q_scale):
    i = pl.program_id(0)
    wo_ref[...] = wo_f32_ref[...].astype(wo_ref.dtype)
    rows = h_ref.shape[0]
    wa = ya_ref.shape[1]
    wb = q_ref.shape[1]
    wcw = yc_ref.shape[1]
    h = h_ref[...]
    col = [0]
    for width in (wa, wa, wb, wb, wb, wcw, wcw, wcw):
        col.append(col[-1] + width)
    proj = lambda n: _dot(h, w_ref[:, col[n]:col[n + 1]])

    u = proj(0)
    vn = _rms(proj(1), gv_ref[...]).astype(BF16)
    t_idx = lax.broadcasted_iota(jnp.int32, (CHUNK, CHUNK), 0)
    s_idx = lax.broadcasted_iota(jnp.int32, (CHUNK, CHUNK), 1)
    for g in range(wa // GROUP_A):
        cs = slice(g * GROUP_A, (g + 1) * GROUP_A)
        ws_g = jnp.where(t_idx >= s_idx, ws_ref[g], 0.0).astype(BF16)
        b_g = bs_ref[:, g:g + 1]
        for c in range(rows // CHUNK):
            rs = slice(c * CHUNK, (c + 1) * CHUNK)
            mixed = _dot(ws_g, vn[rs, cs]) + b_g
            ya_ref[rs, cs] = (u[rs, cs] * mixed).astype(ya_ref.dtype)

    q_ref[...] = (proj(2) * q_scale).astype(q_ref.dtype)
    k_ref[...] = proj(3).astype(k_ref.dtype)
    vt_ref[...] = proj(4).T.astype(vt_ref.dtype)

    bg = proj(5)
    z = proj(6) * proj(7)

    @pl.when(i % tiles_per_seq == 0)
    def _():
        zs_ref[0:HALO, :] = jnp.zeros((HALO, wcw), F32)

    zs_ref[HALO:HALO + rows, :] = z
    w = wconv_ref[...]
    y = zs_ref[HALO - 2:HALO - 2 + rows, :] * w[0:1, :]
    y = y + zs_ref[HALO - 1:HALO - 1 + rows, :] * w[1:2, :]
    y = y + z * w[2:3, :]
    yc_ref[...] = (bg * y).astype(yc_ref.dtype)
    zs_ref[0:HALO, :] = zs_ref[rows:rows + HALO, :]


def _inproj(h, w, gv, ws, bs_t, wconv, w_out, layer, *, seq, wa, wb):
    m, d = h.shape
    rows = ROW_TILE
    wcw = (w.shape[1] - 2 * wa - 3 * wb) // 3
    mix = w_out.shape[1]
    wo_rows = mix // (m // rows)
    assert wo_rows * (m // rows) == mix and wo_rows % 16 == 0
    kern = functools.partial(_inproj_kernel, tiles_per_seq=seq // rows,
                             q_scale=DIFF_HEAD_DIM ** -0.5 * LOG2E)
    lw = lambda i: (layer, 0, 0)
    return pl.pallas_call(
        kern,
        out_shape=[jax.ShapeDtypeStruct((m, wa), BF16),
                   jax.ShapeDtypeStruct((m, wb), BF16),
                   jax.ShapeDtypeStruct((m, wb), BF16),
                   jax.ShapeDtypeStruct((m // rows, wb, rows), BF16),
                   jax.ShapeDtypeStruct((m, wcw), BF16),
                   jax.ShapeDtypeStruct((mix, d), BF16)],
        grid=(m // rows,),
        in_specs=[
            pl.BlockSpec((rows, d), lambda i: (i, 0)),
            _resident(w.shape, lambda i: (0, 0)),
            pl.BlockSpec((None, 1, wa), lw),
            pl.BlockSpec((None, wa // GROUP_A, CHUNK, CHUNK), lambda i: (layer, 0, 0, 0)),
            pl.BlockSpec((None, CHUNK, wa // GROUP_A), lw),
            pl.BlockSpec((None, CONV_WIDTH, wcw), lw),
            pl.BlockSpec((None, wo_rows, d), lambda i: (layer, i, 0)),
        ],
        out_specs=[pl.BlockSpec((rows, wa), lambda i: (i, 0)),
                   pl.BlockSpec((rows, wb), lambda i: (i, 0)),
                   pl.BlockSpec((rows, wb), lambda i: (i, 0)),
                   pl.BlockSpec((None, wb, rows), lambda i: (i, 0, 0)),
                   pl.BlockSpec((rows, wcw), lambda i: (i, 0)),
                   pl.BlockSpec((wo_rows, d), lambda i: (i, 0))],
        scratch_shapes=[pltpu.VMEM((rows + HALO, wcw), F32)],
        compiler_params=_params(1),
        name="inproj_mix_ac",
    )(h, w, gv, ws, bs_t, wconv, w_out)


def _split3(x):
    rnd = lambda v: v.astype(BF16).astype(F32)
    hi = rnd(x)
    mid = rnd(x - hi)
    lo = rnd(x - hi - mid)
    return hi, mid, lo


def _attn_kernel(lq_ref, gs_ref, qa_ref, qb_ref, k_ref, vt_ref, o_ref, biasd_ref, kc_ref, qc_ref, *,
                 lambda_init):
    hd = pl.program_id(1)
    pair = pl.program_id(2)
    nq = k_ref.shape[0]
    t = qa_ref.shape[0]
    half = t // 2
    slope2 = jnp.exp2(-(jnp.full((1, t), hd + 1, jnp.int32).astype(F32))) * LOG2E

    @pl.when(pair == 0)
    def _():
        kk = lax.broadcasted_iota(jnp.int32, (t, t), 0)
        qq = lax.broadcasted_iota(jnp.int32, (t, t), 1)
        biasd_ref[...] = jnp.where(kk <= qq, (kk - qq).astype(F32) * slope2, NEG_BIG)
        r = lax.broadcasted_iota(jnp.int32, (t, V_HEAD_DIM), 0)
        col = jnp.bitwise_and(lax.broadcasted_iota(jnp.int32, (t, V_HEAD_DIM), 1), DIFF_HEAD_DIM - 1)
        rf = r.astype(F32)
        r_hi = jnp.bitwise_and(r, -16).astype(F32)
        kc = jnp.where(col < 3, r_hi, jnp.where(col < 6, rf - r_hi, jnp.where(col < 9, 1.0, 0.0)))
        kc_ref[...] = kc.astype(BF16)
        s_col = jnp.exp2(-(jnp.full((t, V_HEAD_DIM), hd + 1, jnp.int32).astype(F32))) * LOG2E
        s3 = _split3(s_col)
        w3 = _split3(-(s_col * rf))
        qc = jnp.zeros((t, V_HEAD_DIM), F32)
        for n in range(3):
            qc = jnp.where(col == n, s3[n], qc)
            qc = jnp.where(col == n + 3, s3[n], qc)
            qc = jnp.where(col == n + 6, w3[n], qc)
        qc_ref[...] = qc.astype(BF16)

    lane = lax.broadcasted_iota(jnp.int32, (t, V_HEAD_DIM), 1)
    own = (lane < DIFF_HEAD_DIM, lane >= DIFF_HEAD_DIM)
    ones_rows = jnp.ones((16, t), BF16)
    dims = (((1,), (1,)), ((), ()))

    def q_operands(q_ref):
        q = q_ref[...]
        zero = jnp.zeros_like(q)
        return ([jnp.where(own[mp], q, zero) for mp in range(2)],
                [jnp.where(own[mp], q, qc_ref[...]) for mp in range(2)])

    def score(q_ops, j, mp, diag):
        q_plain, q_bias = q_ops
        if diag:
            k = k_ref[j]
            first = lax.dot_general(k[0:half, :], q_plain[mp], dims, preferred_element_type=F32)
            second = lax.dot_general(k[half:, :], q_plain[mp][half:, :], dims, preferred_element_type=F32)
            masked = jnp.full((half, half), NEG_BIG, F32)
            s = jnp.concatenate([first, jnp.concatenate([masked, second], axis=1)], axis=0)
            return s + biasd_ref[...]
        k_aug = jnp.where(own[mp], k_ref[j], kc_ref[...])
        return lax.dot_general(k_aug, q_bias[mp], dims, preferred_element_type=F32)

    def update(state, s, tile_offset, j, diag):
        c = slope2 * float(tile_offset * t)
        m_new = jnp.max(s, axis=0, keepdims=True) + c
        if state is not None:
            m_old, l_old, a_old = state
            m_new = jnp.maximum(m_old, m_new)
        p = jnp.exp2(s - (m_new - c)).astype(BF16)
        vt = jnp.concatenate([vt_ref[j], ones_rows], axis=0)
        if diag:
            pv_second = _dot(vt[:, half:], p[half:, half:])
            pv = _dot(vt[:, 0:half], p[0:half, :]) + jnp.concatenate([jnp.zeros_like(pv_second), pv_second], axis=1)
        else:
            pv = _dot(vt, p)
        l_new, a_new = pv[V_HEAD_DIM:V_HEAD_DIM + 1, :], pv[0:V_HEAD_DIM, :]
        if state is not None:
            alpha = jnp.exp2(m_old - m_new)
            l_new, a_new = l_old * alpha + l_new, a_old * alpha + a_new
        return m_new, l_new, a_new

    def finish(states, slot):
        (_, l1, a1), (_, l2, a2) = states
        lq = lq_ref[...]
        s01 = jnp.sum(lq[0:1, :] * lq[1:2, :], axis=1, keepdims=True)
        s23 = jnp.sum(lq[2:3, :] * lq[3:4, :], axis=1, keepdims=True)
        lam = jnp.exp(s01) - jnp.exp(s23) + lambda_init
        o = a1 * (1.0 / l1) - lam * (a2 * (1.0 / l2))
        y = o * lax.rsqrt(jnp.mean(o * o, axis=0, keepdims=True) + EPS)
        y = y * gs_ref[...] * (1.0 - lambda_init)
        o_ref[slot * t:(slot + 1) * t, :] = y.T.astype(o_ref.dtype)

    def q_pair(first_tile):
        def run():
            q_ops = (q_operands(qa_ref), q_operands(qb_ref))
            work = [(slot, qt, j, mp) for slot, qt in enumerate((first_tile, nq - 1 - first_tile))
                    for j in range(qt + 1) for mp in range(2)]
            issue = lambda w: score(q_ops[w[0]], w[2], w[3], w[2] == w[1])
            states = {}
            pending = [issue(w) for w in work[:SCORE_LOOKAHEAD]]
            for n, (slot, qt, j, mp) in enumerate(work):
                states[slot, mp] = update(states.get((slot, mp)), pending[n], j - qt, j, j == qt)
                if j == qt and mp == 1:
                    finish((states[slot, 0], states[slot, 1]), slot)
                if n + SCORE_LOOKAHEAD < len(work):
                    pending.append(issue(work[n + SCORE_LOOKAHEAD]))
        return run

    for first_tile in range(nq // 2):
        pl.when(pair == first_tile)(q_pair(first_tile))


def _attn(q, k, vt, lq, gsub, layer, *, batch, seq, lambda_init):
    m, wb = q.shape
    t = ROW_TILE
    nq = seq // t
    assert nq % 2 == 0
    heads = wb // V_HEAD_DIM
    k3 = k.reshape(m // t, t, wb)
    kern = functools.partial(_attn_kernel, lambda_init=lambda_init)
    return pl.pallas_call(
        kern,
        out_shape=jax.ShapeDtypeStruct((batch * (nq // 2), 2 * t, wb), BF16),
        grid=(batch, heads, nq // 2),
        in_specs=[
            pl.BlockSpec((None, 4, DIFF_HEAD_DIM), lambda b, h, a: (layer, 0, 0)),
            pl.BlockSpec((None, V_HEAD_DIM, 1), lambda b, h, a: (layer, 0, 0)),
            pl.BlockSpec((t, V_HEAD_DIM), lambda b, h, a: (b * nq + a, h)),
            pl.BlockSpec((t, V_HEAD_DIM), lambda b, h, a: (b * nq + nq - 1 - a, h)),
            pl.BlockSpec((nq, t, V_HEAD_DIM), lambda b, h, a: (b, 0, h)),
            pl.BlockSpec((nq, V_HEAD_DIM, t), lambda b, h, a: (b, h, 0)),
        ],
        out_specs=pl.BlockSpec((None, 2 * t, V_HEAD_DIM), lambda b, h, a: (b * (nq // 2) + a, 0, h)),
        scratch_shapes=[pltpu.VMEM((t, t), F32), pltpu.VMEM((t, V_HEAD_DIM), BF16),
                        pltpu.VMEM((t, V_HEAD_DIM), BF16)],
        compiler_params=_params(3),
        name="diff_attn",
    )(lq, gsub, q, q, k3, vt)


def _outproj_kernel(ya_ref, yb_ref, yc_ref, x_ref, w_ref, gn_ref, xo_ref, hn_ref):
    wa = ya_ref.shape[1]
    wb = yb_ref.shape[1]
    for r0 in range(0, x_ref.shape[0], NORM_ROWS):
        rs = slice(r0, r0 + NORM_ROWS)
        acc = x_ref[rs, :] + _dot(ya_ref[rs, :], w_ref[0:wa, :])
        acc = acc + _dot(yb_ref[rs, :], w_ref[wa:wa + wb, :])
        acc = acc + _dot(yc_ref[rs, :], w_ref[wa + wb:, :])
        xo_ref[rs, :] = acc
        hn_ref[rs, :] = _rms(acc, gn_ref[...]).astype(hn_ref.dtype)


def _outproj(ya, yb_pairs, yc, x, w, gains, layer, *, tiles_per_seq):
    m, d = x.shape
    rows = ROW_TILE
    nq = tiles_per_seq
    wb = yb_pairs.shape[-1]
    yb = yb_pairs.reshape(-1, rows, wb)

    def yb_index(i):
        b, qt = i // nq, i % nq
        a = jnp.minimum(qt, nq - 1 - qt)
        return (2 * (b * (nq // 2) + a) + (qt >= nq // 2).astype(jnp.int32), 0, 0)

    row_block = lambda width: pl.BlockSpec((rows, width), lambda i: (i, 0))
    return pl.pallas_call(
        _outproj_kernel,
        out_shape=[jax.ShapeDtypeStruct((m, d), F32), jax.ShapeDtypeStruct((m, d), BF16)],
        grid=(m // rows,),
        in_specs=[row_block(ya.shape[1]), pl.BlockSpec((None, rows, wb), yb_index), row_block(yc.shape[1]),
                  row_block(d),
                  _resident(w.shape, lambda i: (0, 0)),
                  pl.BlockSpec((None, 1, d), lambda i: (layer, 0, 0))],
        out_specs=[row_block(d), row_block(d)],
        compiler_params=_params(1),
        name="outproj",
    )(ya, yb, yc, x, w, gains)


def kernel(x, g_ffn1, w_ffn1_gate, w_ffn1_up, w_ffn1_down, g_mix, w_in, g_sga_v, w_sga_s, b_sga_s,
           lambda_qk, g_diff_sub, w_conv, w_out, g_ffn2, w_ffn2_gate, w_ffn2_up, w_ffn2_down, g_final):
    batch, seq, d = x.shape
    depth = w_in.shape[0]
    wa = g_sga_v.shape[-1]
    wb = w_out.shape[1] - 2 * wa
    assert seq % ROW_TILE == 0 and wb % V_HEAD_DIM == 0

    cast = lambda w: w.astype(BF16)
    ffn1_f32 = (w_ffn1_gate, w_ffn1_up, w_ffn1_down)
    ffn2_f32 = (w_ffn2_gate, w_ffn2_up, w_ffn2_down)
    ffn_casts = lambda ws, layer: [(ws[0], layer, False), (ws[1], layer, False), (ws[2], layer, True)]
    w_ffn = tuple(cast(w[0]) for w in ffn1_f32)
    w_in_l = cast(w_in[0])
    row = lambda g: g.reshape(g.shape[0], 1, g.shape[-1])
    g1, gm, g2, gv = row(g_ffn1), row(g_mix), row(g_ffn2), row(g_sga_v)
    gf = g_final.reshape(1, 1, d)
    bs_t = jnp.swapaxes(b_sga_s, 1, 2)
    gsub = g_diff_sub.reshape(depth, V_HEAD_DIM, 1)

    xf = x.reshape(batch * seq, d)
    h = _norm(xf, g1, 0)
    for l in range(depth):
        xf, h, *w_ffn = _ffn(h, xf, w_ffn, gm, l, casts=ffn_casts(ffn2_f32, l))
        ya, q, k, vt, yc, wo = _inproj(h, w_in_l, gv, w_sga_s, bs_t, w_conv, w_out, l, seq=seq, wa=wa, wb=wb)
        lambda_init = 0.8 - 0.6 * math.exp(-0.3 * l)
        yb = _attn(q, k, vt, lambda_qk, gsub, l, batch=batch, seq=seq, lambda_init=lambda_init)
        xf, h = _outproj(ya, yb, yc, xf, wo, g2, l, tiles_per_seq=seq // ROW_TILE)
        if l + 1 < depth:
            xf, h, *w_next = _ffn(h, xf, w_ffn, g1, l + 1,
                                  casts=ffn_casts(ffn1_f32, l + 1) + [(w_in, l + 1, False)])
            w_ffn, w_in_l = w_next[0:3], w_next[3]
        else:
            out = _ffn(h, xf, w_ffn, gf, 0)
    return out.reshape(batch, seq, d)
```

```python
import functools
import math

import jax
import jax.numpy as jnp
from jax import lax
from jax.experimental import pallas as pl
from jax.experimental.pallas import tpu as pltpu

F32 = jnp.float32
BF16 = jnp.bfloat16

EPS = 1e-6
CHUNK = 128
GROUP_A = 128
DIFF_HEAD_DIM = 64
V_HEAD_DIM = 2 * DIFF_HEAD_DIM
CONV_WIDTH = 3
HALO = 8
LOG2E = 1.4426950408889634
NEG_BIG = -1e30

V7X_VMEM_LIMIT_BYTES = 60 * 2**20

FFN_ROWS = 1024
FFN_COLS = 512
FFN_X_SLICES = 8
NORM_ROWS = 256
ROW_TILE = 512
SCORE_LOOKAHEAD = 3


def _params(ndim):
    return pltpu.CompilerParams(dimension_semantics=("arbitrary",) * ndim,
                                vmem_limit_bytes=V7X_VMEM_LIMIT_BYTES)


def _dot(a, b):
    return jnp.dot(a, b, preferred_element_type=F32)


def _rms(xf, g):
    y = xf * lax.rsqrt(jnp.mean(xf * xf, axis=-1, keepdims=True) + EPS)
    return y * g


def _resident(block_shape, index_map):
    return pl.BlockSpec(block_shape, index_map, pipeline_mode=pl.Buffered(1))


def _norm_kernel(x_ref, g_ref, o_ref):
    o_ref[...] = _rms(x_ref[...], g_ref[...]).astype(o_ref.dtype)


def _norm(x, gains, layer):
    m, d = x.shape
    return pl.pallas_call(
        _norm_kernel,
        out_shape=jax.ShapeDtypeStruct((m, d), BF16),
        grid=(m // ROW_TILE,),
        in_specs=[pl.BlockSpec((ROW_TILE, d), lambda i: (i, 0)),
                  pl.BlockSpec((None, 1, d), lambda i: (layer, 0, 0))],
        out_specs=pl.BlockSpec((ROW_TILE, d), lambda i: (i, 0)),
        compiler_params=_params(1),
        name="rmsnorm_in",
    )(x, gains)


def _ffn_kernel(h_ref, xs_ref, wg_ref, wu_ref, wd_ref, gn_ref, *refs, nf, rows_xs, n_cast, final):
    cast_src = refs[0:n_cast]
    if final:
        hn_ref, xo_ref = refs[n_cast:]
    else:
        xo_ref, hn_ref = refs[n_cast:n_cast + 2]
        cast_dst = refs[n_cast + 2:]
        for src_ref, dst_ref in zip(cast_src, cast_dst):
            dst_ref[...] = src_ref[...].astype(dst_ref.dtype)
    f = pl.program_id(1)

    @pl.when(f == 0)
    def _():
        xo_ref[...] = jnp.zeros_like(xo_ref)

    @pl.when(f < FFN_X_SLICES)
    def _():
        r0 = pl.multiple_of(f * rows_xs, rows_xs)
        xo_ref[pl.ds(r0, rows_xs), :] += xs_ref[...]

    h = h_ref[...]
    g = _dot(h, wg_ref[...])
    u = _dot(h, wu_ref[...])
    a = ((0.5 * g) * jax.nn.sigmoid(g) * u).astype(BF16)
    xo_ref[...] += _dot(a, wd_ref[...])

    @pl.when(f == nf - 1)
    def _():
        hn_ref[...] = _rms(xo_ref[...], gn_ref[...]).astype(hn_ref.dtype)


def _ffn(h, x, weights, gains, gain_layer, *, casts=None):
    m, d = x.shape
    wg, wu, wd = weights
    d_ff = wg.shape[-1]
    rows = FFN_ROWS
    nf = d_ff // FFN_COLS
    n_i = m // rows
    final = casts is None
    casts = casts or []
    assert nf >= FFN_X_SLICES and rows % (FFN_X_SLICES * 8) == 0
    rows_xs = rows // FFN_X_SLICES
    kern = functools.partial(_ffn_kernel, nf=nf, rows_xs=rows_xs, n_cast=len(casts), final=final)
    row_block = pl.BlockSpec((rows, d), lambda i, f: (i, 0))
    in_specs = [
        row_block,
        pl.BlockSpec((rows_xs, d),
                     lambda i, f: (i * FFN_X_SLICES + jnp.minimum(f, FFN_X_SLICES - 1), 0)),
        pl.BlockSpec((d, FFN_COLS), lambda i, f: (0, f)),
        pl.BlockSpec((d, FFN_COLS), lambda i, f: (0, f)),
        pl.BlockSpec((FFN_COLS, d), lambda i, f: (f, 0)),
        pl.BlockSpec((None, 1, d), lambda i, f: (gain_layer, 0, 0)),
    ]
    operands = [h, x, wg, wu, wd, gains]
    if final:
        out_shape = jax.ShapeDtypeStruct((m, d), F32)
        out_specs = row_block
        scratch = [pltpu.VMEM((rows, d), F32)]
    else:
        out_shape = [jax.ShapeDtypeStruct((m, d), F32), jax.ShapeDtypeStruct((m, d), BF16)]
        out_specs = [row_block, row_block]
        scratch = []
        for w, layer, by_rows in casts:
            _, r, c = w.shape
            if by_rows:
                br, bc = r // (n_i * nf), c
                index = lambda i, f: (i * nf + f, 0)
            else:
                br, bc = r // n_i, c // nf
                index = lambda i, f: (i, f)
            assert r % br == 0 and c % bc == 0 and (r // br) * (c // bc) == n_i * nf
            assert br % 16 == 0 and (bc % 128 == 0 or bc == c)
            in_specs.append(pl.BlockSpec((None, br, bc),
                                         lambda i, f, layer=layer, index=index: (layer,) + index(i, f)))
            operands.append(w)
            out_shape.append(jax.ShapeDtypeStruct((r, c), BF16))
            out_specs.append(pl.BlockSpec((br, bc), index))
    return pl.pallas_call(
        kern,
        out_shape=out_shape,
        grid=(n_i, nf),
        in_specs=in_specs,
        out_specs=out_specs,
        scratch_shapes=scratch,
        compiler_params=_params(2),
        name="ffn_swiglu_final" if final else "ffn_swiglu",
    )(*operands)


def _inproj_kernel(h_ref, w_ref, gv_ref, ws_ref, bs_ref, wconv_ref, wo_f32_ref,
                   ya_ref, q_ref, k_ref, vt_ref, yc_ref, wo_ref, zs_ref, *, tiles_per_seq, q_scale):
    i = pl.program_id(0)
    wo_ref[...] = wo_f32_ref[...].astype(wo_ref.dtype)
    rows = h_ref.shape[0]
    wa = ya_ref.shape[1]
    wb = q_ref.shape[1]
    wcw = yc_ref.shape[1]
    h = h_ref[...]
    col = [0]
    for width in (wa, wa, wb, wb, wb, wcw, wcw, wcw):
        col.append(col[-1] + width)
    proj = lambda n: _dot(h, w_ref[:, col[n]:col[n + 1]])

    u = proj(0)
    vn = _rms(proj(1), gv_ref[...]).astype(BF16)
    t_idx = lax.broadcasted_iota(jnp.int32, (CHUNK, CHUNK), 0)
    s_idx = lax.broadcasted_iota(jnp.int32, (CHUNK, CHUNK), 1)
    for g in range(wa // GROUP_A):
        cs = slice(g * GROUP_A, (g + 1) * GROUP_A)
        ws_g = jnp.where(t_idx >= s_idx, ws_ref[g], 0.0).astype(BF16)
        b_g = bs_ref[:, g:g + 1]
        for c in range(rows // CHUNK):
            rs = slice(c * CHUNK, (c + 1) * CHUNK)
            mixed = _dot(ws_g, vn[rs, cs]) + b_g
            ya_ref[rs, cs] = (u[rs, cs] * mixed).astype(ya_ref.dtype)

    q_ref[...] = (proj(2) * q_scale).astype(q_ref.dtype)
    k_ref[...] = proj(3).astype(k_ref.dtype)
    vt_ref[...] = proj(4).T.astype(vt_ref.dtype)

    bg = proj(5)
    z = proj(6) * proj(7)

    @pl.when(i % tiles_per_seq == 0)
    def _():
        zs_ref[0:HALO, :] = jnp.zeros((HALO, wcw), F32)

    zs_ref[HALO:HALO + rows, :] = z
    w = wconv_ref[...]
    y = zs_ref[HALO - 2:HALO - 2 + rows, :] * w[0:1, :]
    y = y + zs_ref[HALO - 1:HALO - 1 + rows, :] * w[1:2, :]
    y = y + z * w[2:3, :]
    yc_ref[...] = (bg * y).astype(yc_ref.dtype)
    zs_ref[0:HALO, :] = zs_ref[rows:rows + HALO, :]


def _inproj(h, w, gv, ws, bs_t, wconv, w_out, layer, *, seq, wa, wb):
    m, d = h.shape
    rows = ROW_TILE
    wcw = (w.shape[1] - 2 * wa - 3 * wb) // 3
    mix = w_out.shape[1]
    wo_rows = mix // (m // rows)
    assert wo_rows * (m // rows) == mix and wo_rows % 16 == 0
    kern = functools.partial(_inproj_kernel, tiles_per_seq=seq // rows,
                             q_scale=DIFF_HEAD_DIM ** -0.5 * LOG2E)
    lw = lambda i: (layer, 0, 0)
    return pl.pallas_call(
        kern,
        out_shape=[jax.ShapeDtypeStruct((m, wa), BF16),
                   jax.ShapeDtypeStruct((m, wb), BF16),
                   jax.ShapeDtypeStruct((m, wb), BF16),
                   jax.ShapeDtypeStruct((m // rows, wb, rows), BF16),
                   jax.ShapeDtypeStruct((m, wcw), BF16),
                   jax.ShapeDtypeStruct((mix, d), BF16)],
        grid=(m // rows,),
        in_specs=[
            pl.BlockSpec((rows, d), lambda i: (i, 0)),
            _resident(w.shape, lambda i: (0, 0)),
            pl.BlockSpec((None, 1, wa), lw),
            pl.BlockSpec((None, wa // GROUP_A, CHUNK, CHUNK), lambda i: (layer, 0, 0, 0)),
            pl.BlockSpec((None, CHUNK, wa // GROUP_A), lw),
            pl.BlockSpec((None, CONV_WIDTH, wcw), lw),
            pl.BlockSpec((None, wo_rows, d), lambda i: (layer, i, 0)),
        ],
        out_specs=[pl.BlockSpec((rows, wa), lambda i: (i, 0)),
                   pl.BlockSpec((rows, wb), lambda i: (i, 0)),
                   pl.BlockSpec((rows, wb), lambda i: (i, 0)),
                   pl.BlockSpec((None, wb, rows), lambda i: (i, 0, 0)),
                   pl.BlockSpec((rows, wcw), lambda i: (i, 0)),
                   pl.BlockSpec((wo_rows, d), lambda i: (i, 0))],
        scratch_shapes=[pltpu.VMEM((rows + HALO, wcw), F32)],
        compiler_params=_params(1),
        name="inproj_mix_ac",
    )(h, w, gv, ws, bs_t, wconv, w_out)


def _split3(x):
    rnd = lambda v: v.astype(BF16).astype(F32)
    hi = rnd(x)
    mid = rnd(x - hi)
    lo = rnd(x - hi - mid)
    return hi, mid, lo


def _attn_kernel(lq_ref, gs_ref, q_ref, k_ref, vt_ref, o_ref, biasd_ref, kc_ref, qc_ref, *, lambda_init):
    hd = pl.program_id(1)
    nq, t = k_ref.shape[0], k_ref.shape[1]
    half = t // 2
    slope2 = jnp.exp2(-(jnp.full((1, t), hd + 1, jnp.int32).astype(F32))) * LOG2E

    kk = lax.broadcasted_iota(jnp.int32, (t, t), 0)
    qq = lax.broadcasted_iota(jnp.int32, (t, t), 1)
    biasd_ref[...] = jnp.where(kk <= qq, (kk - qq).astype(F32) * slope2, NEG_BIG)
    r = lax.broadcasted_iota(jnp.int32, (t, V_HEAD_DIM), 0)
    col = jnp.bitwise_and(lax.broadcasted_iota(jnp.int32, (t, V_HEAD_DIM), 1), DIFF_HEAD_DIM - 1)
    rf = r.astype(F32)
    r_hi = jnp.bitwise_and(r, -16).astype(F32)
    kc = jnp.where(col < 3, r_hi, jnp.where(col < 6, rf - r_hi, jnp.where(col < 9, 1.0, 0.0)))
    kc_ref[...] = kc.astype(BF16)
    s_col = jnp.exp2(-(jnp.full((t, V_HEAD_DIM), hd + 1, jnp.int32).astype(F32))) * LOG2E
    s3 = _split3(s_col)
    w3 = _split3(-(s_col * rf))
    qc = jnp.zeros((t, V_HEAD_DIM), F32)
    for n in range(3):
        qc = jnp.where(col == n, s3[n], qc)
        qc = jnp.where(col == n + 3, s3[n], qc)
        qc = jnp.where(col == n + 6, w3[n], qc)
    qc_ref[...] = qc.astype(BF16)

    lane = lax.broadcasted_iota(jnp.int32, (t, V_HEAD_DIM), 1)
    own = (lane < DIFF_HEAD_DIM, lane >= DIFF_HEAD_DIM)
    ones_rows = jnp.ones((16, t), BF16)
    dims = (((1,), (1,)), ((), ()))

    def q_operands(qt):
        q = q_ref[qt * t:(qt + 1) * t, :]
        zero = jnp.zeros_like(q)
        return ([jnp.where(own[mp], q, zero) for mp in range(2)],
                [jnp.where(own[mp], q, qc_ref[...]) for mp in range(2)])

    def score(q_ops, j, mp, diag):
        q_plain, q_bias = q_ops
        if diag:
            k = k_ref[j]
            first = lax.dot_general(k[0:half, :], q_plain[mp], dims, preferred_element_type=F32)
            second = lax.dot_general(k[half:, :], q_plain[mp][half:, :], dims, preferred_element_type=F32)
            masked = jnp.full((half, half), NEG_BIG, F32)
            s = jnp.concatenate([first, jnp.concatenate([masked, second], axis=1)], axis=0)
            return s + biasd_ref[...]
        k_aug = jnp.where(own[mp], k_ref[j], kc_ref[...])
        return lax.dot_general(k_aug, q_bias[mp], dims, preferred_element_type=F32)

    def update(state, s, tile_offset, j, diag):
        c = slope2 * float(tile_offset * t)
        m_new = jnp.max(s, axis=0, keepdims=True) + c
        if state is not None:
            m_old, l_old, a_old = state
            m_new = jnp.maximum(m_old, m_new)
        p = jnp.exp2(s - (m_new - c)).astype(BF16)
        vt = jnp.concatenate([vt_ref[j], ones_rows], axis=0)
        if diag:
            pv_second = _dot(vt[:, half:], p[half:, half:])
            pv = _dot(vt[:, 0:half], p[0:half, :]) + jnp.concatenate([jnp.zeros_like(pv_second), pv_second], axis=1)
        else:
            pv = _dot(vt, p)
        l_new, a_new = pv[V_HEAD_DIM:V_HEAD_DIM + 1, :], pv[0:V_HEAD_DIM, :]
        if state is not None:
            alpha = jnp.exp2(m_old - m_new)
            l_new, a_new = l_old * alpha + l_new, a_old * alpha + a_new
        return m_new, l_new, a_new

    def finish(states, qt):
        (_, l1, a1), (_, l2, a2) = states
        lq = lq_ref[...]
        s01 = jnp.sum(lq[0:1, :] * lq[1:2, :], axis=1, keepdims=True)
        s23 = jnp.sum(lq[2:3, :] * lq[3:4, :], axis=1, keepdims=True)
        lam = jnp.exp(s01) - jnp.exp(s23) + lambda_init
        o = a1 * (1.0 / l1) - lam * (a2 * (1.0 / l2))
        y = o * lax.rsqrt(jnp.mean(o * o, axis=0, keepdims=True) + EPS)
        y = y * gs_ref[...] * (1.0 - lambda_init)
        o_ref[qt * t:(qt + 1) * t, :] = y.T.astype(o_ref.dtype)

    q_ops = [q_operands(qt) for qt in range(nq)]
    work = [(qt, j, mp) for qt in range(nq) for j in range(qt + 1) for mp in range(2)]
    issue = lambda w: score(q_ops[w[0]], w[1], w[2], w[1] == w[0])
    states = {}
    pending = [issue(w) for w in work[:SCORE_LOOKAHEAD]]
    for n, (qt, j, mp) in enumerate(work):
        states[qt, mp] = update(states.get((qt, mp)), pending[n], j - qt, j, j == qt)
        if j == qt and mp == 1:
            finish((states.pop((qt, 0)), states.pop((qt, 1))), qt)
        if n + SCORE_LOOKAHEAD < len(work):
            pending.append(issue(work[n + SCORE_LOOKAHEAD]))


def _attn(q, k, vt, lq, gsub, layer, *, batch, seq, lambda_init):
    m, wb = q.shape
    t = ROW_TILE
    nq = seq // t
    heads = wb // V_HEAD_DIM
    k3 = k.reshape(m // t, t, wb)
    kern = functools.partial(_attn_kernel, lambda_init=lambda_init)
    return pl.pallas_call(
        kern,
        out_shape=jax.ShapeDtypeStruct((m, wb), BF16),
        grid=(batch, heads),
        in_specs=[
            pl.BlockSpec((None, 4, DIFF_HEAD_DIM), lambda b, h: (layer, 0, 0)),
            pl.BlockSpec((None, V_HEAD_DIM, 1), lambda b, h: (layer, 0, 0)),
            pl.BlockSpec((seq, V_HEAD_DIM), lambda b, h: (b, h)),
            pl.BlockSpec((nq, t, V_HEAD_DIM), lambda b, h: (b, 0, h)),
            pl.BlockSpec((nq, V_HEAD_DIM, t), lambda b, h: (b, h, 0)),
        ],
        out_specs=pl.BlockSpec((seq, V_HEAD_DIM), lambda b, h: (b, h)),
        scratch_shapes=[pltpu.VMEM((t, t), F32), pltpu.VMEM((t, V_HEAD_DIM), BF16),
                        pltpu.VMEM((t, V_HEAD_DIM), BF16)],
        compiler_params=_params(2),
        name="diff_attn",
    )(lq, gsub, q, k3, vt)


def _outproj_kernel(ya_ref, yb_ref, yc_ref, x_ref, w_ref, gn_ref, xo_ref, hn_ref):
    wa = ya_ref.shape[1]
    wb = yb_ref.shape[1]
    for r0 in range(0, x_ref.shape[0], NORM_ROWS):
        rs = slice(r0, r0 + NORM_ROWS)
        acc = x_ref[rs, :] + _dot(ya_ref[rs, :], w_ref[0:wa, :])
        acc = acc + _dot(yb_ref[rs, :], w_ref[wa:wa + wb, :])
        acc = acc + _dot(yc_ref[rs, :], w_ref[wa + wb:, :])
        xo_ref[rs, :] = acc
        hn_ref[rs, :] = _rms(acc, gn_ref[...]).astype(hn_ref.dtype)


def _outproj(ya, yb, yc, x, w, gains, layer):
    m, d = x.shape
    rows = ROW_TILE
    row_block = lambda width: pl.BlockSpec((rows, width), lambda i: (i, 0))
    return pl.pallas_call(
        _outproj_kernel,
        out_shape=[jax.ShapeDtypeStruct((m, d), F32), jax.ShapeDtypeStruct((m, d), BF16)],
        grid=(m // rows,),
        in_specs=[row_block(ya.shape[1]), row_block(yb.shape[1]), row_block(yc.shape[1]), row_block(d),
                  _resident(w.shape, lambda i: (0, 0)),
                  pl.BlockSpec((None, 1, d), lambda i: (layer, 0, 0))],
        out_specs=[row_block(d), row_block(d)],
        compiler_params=_params(1),
        name="outproj",
    )(ya, yb, yc, x, w, gains)


def kernel(x, g_ffn1, w_ffn1_gate, w_ffn1_up, w_ffn1_down, g_mix, w_in, g_sga_v, w_sga_s, b_sga_s,
           lambda_qk, g_diff_sub, w_conv, w_out, g_ffn2, w_ffn2_gate, w_ffn2_up, w_ffn2_down, g_final):
    batch, seq, d = x.shape
    depth = w_in.shape[0]
    wa = g_sga_v.shape[-1]
    wb = w_out.shape[1] - 2 * wa
    assert seq % ROW_TILE == 0 and wb % V_HEAD_DIM == 0

    cast = lambda w: w.astype(BF16)
    ffn1_f32 = (w_ffn1_gate, w_ffn1_up, w_ffn1_down)
    ffn2_f32 = (w_ffn2_gate, w_ffn2_up, w_ffn2_down)
    ffn_casts = lambda ws, layer: [(ws[0], layer, False), (ws[1], layer, False), (ws[2], layer, True)]
    w_ffn = tuple(cast(w[0]) for w in ffn1_f32)
    w_in_l = cast(w_in[0])
    row = lambda g: g.reshape(g.shape[0], 1, g.shape[-1])
    g1, gm, g2, gv = row(g_ffn1), row(g_mix), row(g_ffn2), row(g_sga_v)
    gf = g_final.reshape(1, 1, d)
    bs_t = jnp.swapaxes(b_sga_s, 1, 2)
    gsub = g_diff_sub.reshape(depth, V_HEAD_DIM, 1)

    xf = x.reshape(batch * seq, d)
    h = _norm(xf, g1, 0)
    for l in range(depth):
        xf, h, *w_ffn = _ffn(h, xf, w_ffn, gm, l, casts=ffn_casts(ffn2_f32, l))
        ya, q, k, vt, yc, wo = _inproj(h, w_in_l, gv, w_sga_s, bs_t, w_conv, w_out, l, seq=seq, wa=wa, wb=wb)
        lambda_init = 0.8 - 0.6 * math.exp(-0.3 * l)
        yb = _attn(q, k, vt, lambda_qk, gsub, l, batch=batch, seq=seq, lambda_init=lambda_init)
        xf, h = _outproj(ya, yb, yc, xf, wo, g2, l)
        if l + 1 < depth:
            xf, h, *w_next = _ffn(h, xf, w_ffn, g1, l + 1,
                                  casts=ffn_casts(ffn1_f32, l + 1) + [(w_in, l + 1, False)])
            w_ffn, w_in_l = w_next[0:3], w_next[3]
        else:
            out = _ffn(h, xf, w_ffn, gf, 0)
    return out.reshape(batch, seq, d)
```

```python
import functools
import math

import jax
import jax.numpy as jnp
from jax import lax
from jax.experimental import pallas as pl
from jax.experimental.pallas import tpu as pltpu

F32 = jnp.float32
BF16 = jnp.bfloat16

EPS = 1e-6
CHUNK = 128
GROUP_A = 128
DIFF_HEAD_DIM = 64
V_HEAD_DIM = 2 * DIFF_HEAD_DIM
CONV_WIDTH = 3
HALO = 8
LOG2E = 1.4426950408889634
NEG_BIG = -1e30

V7X_VMEM_LIMIT_BYTES = 60 * 2**20

FFN_ROWS = 1024
FFN_COLS = 512
FFN_X_SLICES = 8
NORM_ROWS = 256
ROW_TILE = 512
SCORE_LOOKAHEAD = 6


def _params(ndim):
    return pltpu.CompilerParams(dimension_semantics=("arbitrary",) * ndim,
                                vmem_limit_bytes=V7X_VMEM_LIMIT_BYTES)


def _dot(a, b):
    return jnp.dot(a, b, preferred_element_type=F32)


def _rms(xf, g):
    y = xf * lax.rsqrt(jnp.mean(xf * xf, axis=-1, keepdims=True) + EPS)
    return y * g


def _resident(block_shape, index_map):
    return pl.BlockSpec(block_shape, index_map, pipeline_mode=pl.Buffered(1))


def _norm_kernel(x_ref, g_ref, o_ref):
    o_ref[...] = _rms(x_ref[...], g_ref[...]).astype(o_ref.dtype)


def _norm(x, gains, layer):
    m, d = x.shape
    return pl.pallas_call(
        _norm_kernel,
        out_shape=jax.ShapeDtypeStruct((m, d), BF16),
        grid=(m // ROW_TILE,),
        in_specs=[pl.BlockSpec((ROW_TILE, d), lambda i: (i, 0)),
                  pl.BlockSpec((None, 1, d), lambda i: (layer, 0, 0))],
        out_specs=pl.BlockSpec((ROW_TILE, d), lambda i: (i, 0)),
        compiler_params=_params(1),
        name="rmsnorm_in",
    )(x, gains)


def _ffn_kernel(h_ref, xs_ref, wg_ref, wu_ref, wd_ref, gn_ref, *refs, nf, rows_xs, n_cast, final):
    cast_src = refs[0:n_cast]
    if final:
        hn_ref, xo_ref = refs[n_cast:]
    else:
        xo_ref, hn_ref = refs[n_cast:n_cast + 2]
        cast_dst = refs[n_cast + 2:]
        for src_ref, dst_ref in zip(cast_src, cast_dst):
            dst_ref[...] = src_ref[...].astype(dst_ref.dtype)
    f = pl.program_id(1)

    @pl.when(f == 0)
    def _():
        xo_ref[...] = jnp.zeros_like(xo_ref)

    @pl.when(f < FFN_X_SLICES)
    def _():
        r0 = pl.multiple_of(f * rows_xs, rows_xs)
        xo_ref[pl.ds(r0, rows_xs), :] += xs_ref[...]

    h = h_ref[...]
    g = _dot(h, wg_ref[...])
    u = _dot(h, wu_ref[...])
    a = ((0.5 * g) * jax.nn.sigmoid(g) * u).astype(BF16)
    xo_ref[...] += _dot(a, wd_ref[...])

    @pl.when(f == nf - 1)
    def _():
        hn_ref[...] = _rms(xo_ref[...], gn_ref[...]).astype(hn_ref.dtype)


def _ffn(h, x, weights, gains, gain_layer, *, casts=None):
    m, d = x.shape
    wg, wu, wd = weights
    d_ff = wg.shape[-1]
    rows = FFN_ROWS
    nf = d_ff // FFN_COLS
    n_i = m // rows
    final = casts is None
    casts = casts or []
    assert nf >= FFN_X_SLICES and rows % (FFN_X_SLICES * 8) == 0
    rows_xs = rows // FFN_X_SLICES
    kern = functools.partial(_ffn_kernel, nf=nf, rows_xs=rows_xs, n_cast=len(casts), final=final)
    row_block = pl.BlockSpec((rows, d), lambda i, f: (i, 0))
    in_specs = [
        row_block,
        pl.BlockSpec((rows_xs, d),
                     lambda i, f: (i * FFN_X_SLICES + jnp.minimum(f, FFN_X_SLICES - 1), 0)),
        pl.BlockSpec((d, FFN_COLS), lambda i, f: (0, f)),
        pl.BlockSpec((d, FFN_COLS), lambda i, f: (0, f)),
        pl.BlockSpec((FFN_COLS, d), lambda i, f: (f, 0)),
        pl.BlockSpec((None, 1, d), lambda i, f: (gain_layer, 0, 0)),
    ]
    operands = [h, x, wg, wu, wd, gains]
    if final:
        out_shape = jax.ShapeDtypeStruct((m, d), F32)
        out_specs = row_block
        scratch = [pltpu.VMEM((rows, d), F32)]
    else:
        out_shape = [jax.ShapeDtypeStruct((m, d), F32), jax.ShapeDtypeStruct((m, d), BF16)]
        out_specs = [row_block, row_block]
        scratch = []
        for w, layer, by_rows in casts:
            _, r, c = w.shape
            if by_rows:
                br, bc = r // (n_i * nf), c
                index = lambda i, f: (i * nf + f, 0)
            else:
                br, bc = r // n_i, c // nf
                index = lambda i, f: (i, f)
            assert r % br == 0 and c % bc == 0 and (r // br) * (c // bc) == n_i * nf
            assert br % 16 == 0 and (bc % 128 == 0 or bc == c)
            in_specs.append(pl.BlockSpec((None, br, bc),
                                         lambda i, f, layer=layer, index=index: (layer,) + index(i, f)))
            operands.append(w)
            out_shape.append(jax.ShapeDtypeStruct((r, c), BF16))
            out_specs.append(pl.BlockSpec((br, bc), index))
    return pl.pallas_call(
        kern,
        out_shape=out_shape,
        grid=(n_i, nf),
        in_specs=in_specs,
        out_specs=out_specs,
        scratch_shapes=scratch,
        compiler_params=_params(2),
        name="ffn_swiglu_final" if final else "ffn_swiglu",
    )(*operands)


def _inproj_kernel(h_ref, w_ref, gv_ref, ws_ref, bs_ref, wconv_ref, wo_f32_ref,
                   ya_ref, q_ref, k_ref, vt_ref, yc_ref, wo_ref, zs_ref, *, tiles_per_seq, q_scale):
    i = pl.program_id(0)
    wo_ref[...] = wo_f32_ref[...].astype(wo_ref.dtype)
    rows = h_ref.shape[0]
    wa = ya_ref.shape[1]
    wb = q_ref.shape[1]
    wcw = yc_ref.shape[1]
    h = h_ref[...]
    col = [0]
    for width in (wa, wa, wb, wb, wb, wcw, wcw, wcw):
        col.append(col[-1] + width)
    proj = lambda n: _dot(h, w_ref[:, col[n]:col[n + 1]])

    u = proj(0)
    vn = _rms(proj(1), gv_ref[...]).astype(BF16)
    t_idx = lax.broadcasted_iota(jnp.int32, (CHUNK, CHUNK), 0)
    s_idx = lax.broadcasted_iota(jnp.int32, (CHUNK, CHUNK), 1)
    for g in range(wa // GROUP_A):
        cs = slice(g * GROUP_A, (g + 1) * GROUP_A)
        ws_g = jnp.where(t_idx >= s_idx, ws_ref[g], 0.0).astype(BF16)
        b_g = bs_ref[:, g:g + 1]
        for c in range(rows // CHUNK):
            rs = slice(c * CHUNK, (c + 1) * CHUNK)
            mixed = _dot(ws_g, vn[rs, cs]) + b_g
            ya_ref[rs, cs] = (u[rs, cs] * mixed).astype(ya_ref.dtype)

    q_ref[...] = (proj(2) * q_scale).astype(q_ref.dtype)
    k_ref[...] = proj(3).astype(k_ref.dtype)
    vt_ref[...] = proj(4).T.astype(vt_ref.dtype)

    bg = proj(5)
    z = proj(6) * proj(7)

    @pl.when(i % tiles_per_seq == 0)
    def _():
        zs_ref[0:HALO, :] = jnp.zeros((HALO, wcw), F32)

    zs_ref[HALO:HALO + rows, :] = z
    w = wconv_ref[...]
    y = zs_ref[HALO - 2:HALO - 2 + rows, :] * w[0:1, :]
    y = y + zs_ref[HALO - 1:HALO - 1 + rows, :] * w[1:2, :]
    y = y + z * w[2:3, :]
    yc_ref[...] = (bg * y).astype(yc_ref.dtype)
    zs_ref[0:HALO, :] = zs_ref[rows:rows + HALO, :]


def _inproj(h, w, gv, ws, bs_t, wconv, w_out, layer, *, seq, wa, wb):
    m, d = h.shape
    rows = ROW_TILE
    wcw = (w.shape[1] - 2 * wa - 3 * wb) // 3
    mix = w_out.shape[1]
    wo_rows = mix // (m // rows)
    assert wo_rows * (m // rows) == mix and wo_rows % 16 == 0
    kern = functools.partial(_inproj_kernel, tiles_per_seq=seq // rows,
                             q_scale=DIFF_HEAD_DIM ** -0.5 * LOG2E)
    lw = lambda i: (layer, 0, 0)
    return pl.pallas_call(
        kern,
        out_shape=[jax.ShapeDtypeStruct((m, wa), BF16),
                   jax.ShapeDtypeStruct((m, wb), BF16),
                   jax.ShapeDtypeStruct((m, wb), BF16),
                   jax.ShapeDtypeStruct((m // rows, wb, rows), BF16),
                   jax.ShapeDtypeStruct((m, wcw), BF16),
                   jax.ShapeDtypeStruct((mix, d), BF16)],
        grid=(m // rows,),
        in_specs=[
            pl.BlockSpec((rows, d), lambda i: (i, 0)),
            _resident(w.shape, lambda i: (0, 0)),
            pl.BlockSpec((None, 1, wa), lw),
            pl.BlockSpec((None, wa // GROUP_A, CHUNK, CHUNK), lambda i: (layer, 0, 0, 0)),
            pl.BlockSpec((None, CHUNK, wa // GROUP_A), lw),
            pl.BlockSpec((None, CONV_WIDTH, wcw), lw),
            pl.BlockSpec((None, wo_rows, d), lambda i: (layer, i, 0)),
        ],
        out_specs=[pl.BlockSpec((rows, wa), lambda i: (i, 0)),
                   pl.BlockSpec((rows, wb), lambda i: (i, 0)),
                   pl.BlockSpec((rows, wb), lambda i: (i, 0)),
                   pl.BlockSpec((None, wb, rows), lambda i: (i, 0, 0)),
                   pl.BlockSpec((rows, wcw), lambda i: (i, 0)),
                   pl.BlockSpec((wo_rows, d), lambda i: (i, 0))],
        scratch_shapes=[pltpu.VMEM((rows + HALO, wcw), F32)],
        compiler_params=_params(1),
        name="inproj_mix_ac",
    )(h, w, gv, ws, bs_t, wconv, w_out)


def _split3(x):
    rnd = lambda v: v.astype(BF16).astype(F32)
    hi = rnd(x)
    mid = rnd(x - hi)
    lo = rnd(x - hi - mid)
    return hi, mid, lo


def _attn_kernel(lq_ref, gs_ref, q_ref, k_ref, vt_ref, o_ref, biasd_ref, kc_ref, qc_ref, *, lambda_init):
    hd = pl.program_id(1)
    nq, t = k_ref.shape[0], k_ref.shape[1]
    half = t // 2
    slope2 = jnp.exp2(-(jnp.full((1, t), hd + 1, jnp.int32).astype(F32))) * LOG2E

    kk = lax.broadcasted_iota(jnp.int32, (t, t), 0)
    qq = lax.broadcasted_iota(jnp.int32, (t, t), 1)
    biasd_ref[...] = jnp.where(kk <= qq, (kk - qq).astype(F32) * slope2, NEG_BIG)
    r = lax.broadcasted_iota(jnp.int32, (t, V_HEAD_DIM), 0)
    col = jnp.bitwise_and(lax.broadcasted_iota(jnp.int32, (t, V_HEAD_DIM), 1), DIFF_HEAD_DIM - 1)
    rf = r.astype(F32)
    r_hi = jnp.bitwise_and(r, -16).astype(F32)
    kc = jnp.where(col < 3, r_hi, jnp.where(col < 6, rf - r_hi, jnp.where(col < 9, 1.0, 0.0)))
    kc_ref[...] = kc.astype(BF16)
    s_col = jnp.exp2(-(jnp.full((t, V_HEAD_DIM), hd + 1, jnp.int32).astype(F32))) * LOG2E
    s3 = _split3(s_col)
    w3 = _split3(-(s_col * rf))
    qc = jnp.zeros((t, V_HEAD_DIM), F32)
    for n in range(3):
        qc = jnp.where(col == n, s3[n], qc)
        qc = jnp.where(col == n + 3, s3[n], qc)
        qc = jnp.where(col == n + 6, w3[n], qc)
    qc_ref[...] = qc.astype(BF16)

    lane = lax.broadcasted_iota(jnp.int32, (t, V_HEAD_DIM), 1)
    own = (lane < DIFF_HEAD_DIM, lane >= DIFF_HEAD_DIM)
    ones_rows = jnp.ones((16, t), BF16)
    dims = (((1,), (1,)), ((), ()))

    def q_operands(qt):
        q = q_ref[qt * t:(qt + 1) * t, :]
        zero = jnp.zeros_like(q)
        return ([jnp.where(own[mp], q, zero) for mp in range(2)],
                [jnp.where(own[mp], q, qc_ref[...]) for mp in range(2)])

    def score(q_ops, j, mp, diag):
        q_plain, q_bias = q_ops
        if diag:
            k = k_ref[j]
            first = lax.dot_general(k[0:half, :], q_plain[mp], dims, preferred_element_type=F32)
            second = lax.dot_general(k[half:, :], q_plain[mp][half:, :], dims, preferred_element_type=F32)
            masked = jnp.full((half, half), NEG_BIG, F32)
            s = jnp.concatenate([first, jnp.concatenate([masked, second], axis=1)], axis=0)
            return s + biasd_ref[...]
        k_aug = jnp.where(own[mp], k_ref[j], kc_ref[...])
        return lax.dot_general(k_aug, q_bias[mp], dims, preferred_element_type=F32)

    def update(state, s, tile_offset, j, diag):
        c = slope2 * float(tile_offset * t)
        m_new = jnp.max(s, axis=0, keepdims=True) + c
        if state is not None:
            m_old, l_old, a_old = state
            m_new = jnp.maximum(m_old, m_new)
        p = jnp.exp2(s - (m_new - c)).astype(BF16)
        vt = jnp.concatenate([vt_ref[j], ones_rows], axis=0)
        if diag:
            pv_second = _dot(vt[:, half:], p[half:, half:])
            pv = _dot(vt[:, 0:half], p[0:half, :]) + jnp.concatenate([jnp.zeros_like(pv_second), pv_second], axis=1)
        else:
            pv = _dot(vt, p)
        l_new, a_new = pv[V_HEAD_DIM:V_HEAD_DIM + 1, :], pv[0:V_HEAD_DIM, :]
        if state is not None:
            alpha = jnp.exp2(m_old - m_new)
            l_new, a_new = l_old * alpha + l_new, a_old * alpha + a_new
        return m_new, l_new, a_new

    def finish(states, qt):
        (_, l1, a1), (_, l2, a2) = states
        lq = lq_ref[...]
        s01 = jnp.sum(lq[0:1, :] * lq[1:2, :], axis=1, keepdims=True)
        s23 = jnp.sum(lq[2:3, :] * lq[3:4, :], axis=1, keepdims=True)
        lam = jnp.exp(s01) - jnp.exp(s23) + lambda_init
        o = a1 * (1.0 / l1) - lam * (a2 * (1.0 / l2))
        y = o * lax.rsqrt(jnp.mean(o * o, axis=0, keepdims=True) + EPS)
        y = y * gs_ref[...] * (1.0 - lambda_init)
        o_ref[qt * t:(qt + 1) * t, :] = y.T.astype(o_ref.dtype)

    q_ops = [q_operands(qt) for qt in range(nq)]
    work = [(qt, j, mp) for qt in range(nq) for j in range(qt + 1) for mp in range(2)]
    issue = lambda w: score(q_ops[w[0]], w[1], w[2], w[1] == w[0])
    states = {}
    pending = [issue(w) for w in work[:SCORE_LOOKAHEAD]]
    for n, (qt, j, mp) in enumerate(work):
        states[qt, mp] = update(states.get((qt, mp)), pending[n], j - qt, j, j == qt)
        if j == qt and mp == 1:
            finish((states.pop((qt, 0)), states.pop((qt, 1))), qt)
        if n + SCORE_LOOKAHEAD < len(work):
            pending.append(issue(work[n + SCORE_LOOKAHEAD]))


def _attn(q, k, vt, lq, gsub, layer, *, batch, seq, lambda_init):
    m, wb = q.shape
    t = ROW_TILE
    nq = seq // t
    heads = wb // V_HEAD_DIM
    k3 = k.reshape(m // t, t, wb)
    kern = functools.partial(_attn_kernel, lambda_init=lambda_init)
    return pl.pallas_call(
        kern,
        out_shape=jax.ShapeDtypeStruct((m, wb), BF16),
        grid=(batch, heads),
        in_specs=[
            pl.BlockSpec((None, 4, DIFF_HEAD_DIM), lambda b, h: (layer, 0, 0)),
            pl.BlockSpec((None, V_HEAD_DIM, 1), lambda b, h: (layer, 0, 0)),
            pl.BlockSpec((seq, V_HEAD_DIM), lambda b, h: (b, h)),
            pl.BlockSpec((nq, t, V_HEAD_DIM), lambda b, h: (b, 0, h)),
            pl.BlockSpec((nq, V_HEAD_DIM, t), lambda b, h: (b, h, 0)),
        ],
        out_specs=pl.BlockSpec((seq, V_HEAD_DIM), lambda b, h: (b, h)),
        scratch_shapes=[pltpu.VMEM((t, t), F32), pltpu.VMEM((t, V_HEAD_DIM), BF16),
                        pltpu.VMEM((t, V_HEAD_DIM), BF16)],
        compiler_params=_params(2),
        name="diff_attn",
    )(lq, gsub, q, k3, vt)


def _outproj_kernel(ya_ref, yb_ref, yc_ref, x_ref, w_ref, gn_ref, xo_ref, hn_ref):
    wa = ya_ref.shape[1]
    wb = yb_ref.shape[1]
    for r0 in range(0, x_ref.shape[0], NORM_ROWS):
        rs = slice(r0, r0 + NORM_ROWS)
        acc = x_ref[rs, :] + _dot(ya_ref[rs, :], w_ref[0:wa, :])
        acc = acc + _dot(yb_ref[rs, :], w_ref[wa:wa + wb, :])
        acc = acc + _dot(yc_ref[rs, :], w_ref[wa + wb:, :])
        xo_ref[rs, :] = acc
        hn_ref[rs, :] = _rms(acc, gn_ref[...]).astype(hn_ref.dtype)


def _outproj(ya, yb, yc, x, w, gains, layer):
    m, d = x.shape
    rows = ROW_TILE
    row_block = lambda width: pl.BlockSpec((rows, width), lambda i: (i, 0))
    return pl.pallas_call(
        _outproj_kernel,
        out_shape=[jax.ShapeDtypeStruct((m, d), F32), jax.ShapeDtypeStruct((m, d), BF16)],
        grid=(m // rows,),
        in_specs=[row_block(ya.shape[1]), row_block(yb.shape[1]), row_block(yc.shape[1]), row_block(d),
                  _resident(w.shape, lambda i: (0, 0)),
                  pl.BlockSpec((None, 1, d), lambda i: (layer, 0, 0))],
        out_specs=[row_block(d), row_block(d)],
        compiler_params=_params(1),
        name="outproj",
    )(ya, yb, yc, x, w, gains)


def kernel(x, g_ffn1, w_ffn1_gate, w_ffn1_up, w_ffn1_down, g_mix, w_in, g_sga_v, w_sga_s, b_sga_s,
           lambda_qk, g_diff_sub, w_conv, w_out, g_ffn2, w_ffn2_gate, w_ffn2_up, w_ffn2_down, g_final):
    batch, seq, d = x.shape
    depth = w_in.shape[0]
    wa = g_sga_v.shape[-1]
    wb = w_out.shape[1] - 2 * wa
    assert seq % ROW_TILE == 0 and wb % V_HEAD_DIM == 0

    cast = lambda w: w.astype(BF16)
    ffn1_f32 = (w_ffn1_gate, w_ffn1_up, w_ffn1_down)
    ffn2_f32 = (w_ffn2_gate, w_ffn2_up, w_ffn2_down)
    ffn_casts = lambda ws, layer: [(ws[0], layer, False), (ws[1], layer, False), (ws[2], layer, True)]
    w_ffn = tuple(cast(w[0]) for w in ffn1_f32)
    w_in_l = cast(w_in[0])
    row = lambda g: g.reshape(g.shape[0], 1, g.shape[-1])
    g1, gm, g2, gv = row(g_ffn1), row(g_mix), row(g_ffn2), row(g_sga_v)
    gf = g_final.reshape(1, 1, d)
    bs_t = jnp.swapaxes(b_sga_s, 1, 2)
    gsub = g_diff_sub.reshape(depth, V_HEAD_DIM, 1)

    xf = x.reshape(batch * seq, d)
    h = _norm(xf, g1, 0)
    for l in range(depth):
        xf, h, *w_ffn = _ffn(h, xf, w_ffn, gm, l, casts=ffn_casts(ffn2_f32, l))
        ya, q, k, vt, yc, wo = _inproj(h, w_in_l, gv, w_sga_s, bs_t, w_conv, w_out, l, seq=seq, wa=wa, wb=wb)
        lambda_init = 0.8 - 0.6 * math.exp(-0.3 * l)
        yb = _attn(q, k, vt, lambda_qk, gsub, l, batch=batch, seq=seq, lambda_init=lambda_init)
        xf, h = _outproj(ya, yb, yc, xf, wo, g2, l)
        if l + 1 < depth:
            xf, h, *w_next = _ffn(h, xf, w_ffn, g1, l + 1,
                                  casts=ffn_casts(ffn1_f32, l + 1) + [(w_in, l + 1, False)])
            w_ffn, w_in_l = w_next[0:3], w_next[3]
        else:
            out = _ffn(h, xf, w_ffn, gf, 0)
    return out.reshape(batch, seq, d)
```

```python
import functools
import math

import jax
import jax.numpy as jnp
from jax import lax
from jax.experimental import pallas as pl
from jax.experimental.pallas import tpu as pltpu

F32 = jnp.float32
BF16 = jnp.bfloat16

EPS = 1e-6
CHUNK = 128
GROUP_A = 128
DIFF_HEAD_DIM = 64
V_HEAD_DIM = 2 * DIFF_HEAD_DIM
CONV_WIDTH = 3
HALO = 8
LOG2E = 1.4426950408889634
NEG_BIG = -1e30

V7X_VMEM_LIMIT_BYTES = 60 * 2**20

FFN_ROWS = 1024
FFN_COLS = 512
FFN_X_SLICES = 8
NORM_ROWS = 256
ROW_TILE = 512
SCORE_LOOKAHEAD = 8


def _params(ndim):
    return pltpu.CompilerParams(dimension_semantics=("arbitrary",) * ndim,
                                vmem_limit_bytes=V7X_VMEM_LIMIT_BYTES)


def _dot(a, b):
    return jnp.dot(a, b, preferred_element_type=F32)


def _rms(xf, g):
    y = xf * lax.rsqrt(jnp.mean(xf * xf, axis=-1, keepdims=True) + EPS)
    return y * g


def _resident(block_shape, index_map):
    return pl.BlockSpec(block_shape, index_map, pipeline_mode=pl.Buffered(1))


def _norm_kernel(x_ref, g_ref, o_ref):
    o_ref[...] = _rms(x_ref[...], g_ref[...]).astype(o_ref.dtype)


def _norm(x, gains, layer):
    m, d = x.shape
    return pl.pallas_call(
        _norm_kernel,
        out_shape=jax.ShapeDtypeStruct((m, d), BF16),
        grid=(m // ROW_TILE,),
        in_specs=[pl.BlockSpec((ROW_TILE, d), lambda i: (i, 0)),
                  pl.BlockSpec((None, 1, d), lambda i: (layer, 0, 0))],
        out_specs=pl.BlockSpec((ROW_TILE, d), lambda i: (i, 0)),
        compiler_params=_params(1),
        name="rmsnorm_in",
    )(x, gains)


def _ffn_kernel(h_ref, xs_ref, wg_ref, wu_ref, wd_ref, gn_ref, *refs, nf, rows_xs, n_cast, final):
    cast_src = refs[0:n_cast]
    if final:
        hn_ref, xo_ref = refs[n_cast:]
    else:
        xo_ref, hn_ref = refs[n_cast:n_cast + 2]
        cast_dst = refs[n_cast + 2:]
        for src_ref, dst_ref in zip(cast_src, cast_dst):
            dst_ref[...] = src_ref[...].astype(dst_ref.dtype)
    f = pl.program_id(1)

    @pl.when(f == 0)
    def _():
        xo_ref[...] = jnp.zeros_like(xo_ref)

    @pl.when(f < FFN_X_SLICES)
    def _():
        r0 = pl.multiple_of(f * rows_xs, rows_xs)
        xo_ref[pl.ds(r0, rows_xs), :] += xs_ref[...]

    h = h_ref[...]
    g = _dot(h, wg_ref[...])
    u = _dot(h, wu_ref[...])
    a = ((0.5 * g) * jax.nn.sigmoid(g) * u).astype(BF16)
    xo_ref[...] += _dot(a, wd_ref[...])

    @pl.when(f == nf - 1)
    def _():
        hn_ref[...] = _rms(xo_ref[...], gn_ref[...]).astype(hn_ref.dtype)


def _ffn(h, x, weights, gains, gain_layer, *, casts=None):
    m, d = x.shape
    wg, wu, wd = weights
    d_ff = wg.shape[-1]
    rows = FFN_ROWS
    nf = d_ff // FFN_COLS
    n_i = m // rows
    final = casts is None
    casts = casts or []
    assert nf >= FFN_X_SLICES and rows % (FFN_X_SLICES * 8) == 0
    rows_xs = rows // FFN_X_SLICES
    kern = functools.partial(_ffn_kernel, nf=nf, rows_xs=rows_xs, n_cast=len(casts), final=final)
    row_block = pl.BlockSpec((rows, d), lambda i, f: (i, 0))
    in_specs = [
        row_block,
        pl.BlockSpec((rows_xs, d),
                     lambda i, f: (i * FFN_X_SLICES + jnp.minimum(f, FFN_X_SLICES - 1), 0)),
        pl.BlockSpec((d, FFN_COLS), lambda i, f: (0, f)),
        pl.BlockSpec((d, FFN_COLS), lambda i, f: (0, f)),
        pl.BlockSpec((FFN_COLS, d), lambda i, f: (f, 0)),
        pl.BlockSpec((None, 1, d), lambda i, f: (gain_layer, 0, 0)),
    ]
    operands = [h, x, wg, wu, wd, gains]
    if final:
        out_shape = jax.ShapeDtypeStruct((m, d), F32)
        out_specs = row_block
        scratch = [pltpu.VMEM((rows, d), F32)]
    else:
        out_shape = [jax.ShapeDtypeStruct((m, d), F32), jax.ShapeDtypeStruct((m, d), BF16)]
        out_specs = [row_block, row_block]
        scratch = []
        for w, layer, by_rows in casts:
            _, r, c = w.shape
            if by_rows:
                br, bc = r // (n_i * nf), c
                index = lambda i, f: (i * nf + f, 0)
            else:
                br, bc = r // n_i, c // nf
                index = lambda i, f: (i, f)
            assert r % br == 0 and c % bc == 0 and (r // br) * (c // bc) == n_i * nf
            assert br % 16 == 0 and (bc % 128 == 0 or bc == c)
            in_specs.append(pl.BlockSpec((None, br, bc),
                                         lambda i, f, layer=layer, index=index: (layer,) + index(i, f)))
            operands.append(w)
            out_shape.append(jax.ShapeDtypeStruct((r, c), BF16))
            out_specs.append(pl.BlockSpec((br, bc), index))
    return pl.pallas_call(
        kern,
        out_shape=out_shape,
        grid=(n_i, nf),
        in_specs=in_specs,
        out_specs=out_specs,
        scratch_shapes=scratch,
        compiler_params=_params(2),
        name="ffn_swiglu_final" if final else "ffn_swiglu",
    )(*operands)


def _inproj_kernel(h_ref, w_ref, gv_ref, ws_ref, bs_ref, wconv_ref, wo_f32_ref,
                   ya_ref, q_ref, k_ref, vt_ref, yc_ref, wo_ref, zs_ref, *, tiles_per_seq, q_scale):
    i = pl.program_id(0)
    wo_ref[...] = wo_f32_ref[...].astype(wo_ref.dtype)
    rows = h_ref.shape[0]
    wa = ya_ref.shape[1]
    wb = q_ref.shape[1]
    wcw = yc_ref.shape[1]
    h = h_ref[...]
    col = [0]
    for width in (wa, wa, wb, wb, wb, wcw, wcw, wcw):
        col.append(col[-1] + width)
    proj = lambda n: _dot(h, w_ref[:, col[n]:col[n + 1]])

    u = proj(0)
    vn = _rms(proj(1), gv_ref[...]).astype(BF16)
    t_idx = lax.broadcasted_iota(jnp.int32, (CHUNK, CHUNK), 0)
    s_idx = lax.broadcasted_iota(jnp.int32, (CHUNK, CHUNK), 1)
    for g in range(wa // GROUP_A):
        cs = slice(g * GROUP_A, (g + 1) * GROUP_A)
        ws_g = jnp.where(t_idx >= s_idx, ws_ref[g], 0.0).astype(BF16)
        b_g = bs_ref[:, g:g + 1]
        for c in range(rows // CHUNK):
            rs = slice(c * CHUNK, (c + 1) * CHUNK)
            mixed = _dot(ws_g, vn[rs, cs]) + b_g
            ya_ref[rs, cs] = (u[rs, cs] * mixed).astype(ya_ref.dtype)

    q_ref[...] = (proj(2) * q_scale).astype(q_ref.dtype)
    k_ref[...] = proj(3).astype(k_ref.dtype)
    vt_ref[...] = proj(4).T.astype(vt_ref.dtype)

    bg = proj(5)
    z = proj(6) * proj(7)

    @pl.when(i % tiles_per_seq == 0)
    def _():
        zs_ref[0:HALO, :] = jnp.zeros((HALO, wcw), F32)

    zs_ref[HALO:HALO + rows, :] = z
    w = wconv_ref[...]
    y = zs_ref[HALO - 2:HALO - 2 + rows, :] * w[0:1, :]
    y = y + zs_ref[HALO - 1:HALO - 1 + rows, :] * w[1:2, :]
    y = y + z * w[2:3, :]
    yc_ref[...] = (bg * y).astype(yc_ref.dtype)
    zs_ref[0:HALO, :] = zs_ref[rows:rows + HALO, :]


def _inproj(h, w, gv, ws, bs_t, wconv, w_out, layer, *, seq, wa, wb):
    m, d = h.shape
    rows = ROW_TILE
    wcw = (w.shape[1] - 2 * wa - 3 * wb) // 3
    mix = w_out.shape[1]
    wo_rows = mix // (m // rows)
    assert wo_rows * (m // rows) == mix and wo_rows % 16 == 0
    kern = functools.partial(_inproj_kernel, tiles_per_seq=seq // rows,
                             q_scale=DIFF_HEAD_DIM ** -0.5 * LOG2E)
    lw = lambda i: (layer, 0, 0)
    return pl.pallas_call(
        kern,
        out_shape=[jax.ShapeDtypeStruct((m, wa), BF16),
                   jax.ShapeDtypeStruct((m, wb), BF16),
                   jax.ShapeDtypeStruct((m, wb), BF16),
                   jax.ShapeDtypeStruct((m // rows, wb, rows), BF16),
                   jax.ShapeDtypeStruct((m, wcw), BF16),
                   jax.ShapeDtypeStruct((mix, d), BF16)],
        grid=(m // rows,),
        in_specs=[
            pl.BlockSpec((rows, d), lambda i: (i, 0)),
            _resident(w.shape, lambda i: (0, 0)),
            pl.BlockSpec((None, 1, wa), lw),
            pl.BlockSpec((None, wa // GROUP_A, CHUNK, CHUNK), lambda i: (layer, 0, 0, 0)),
            pl.BlockSpec((None, CHUNK, wa // GROUP_A), lw),
            pl.BlockSpec((None, CONV_WIDTH, wcw), lw),
            pl.BlockSpec((None, wo_rows, d), lambda i: (layer, i, 0)),
        ],
        out_specs=[pl.BlockSpec((rows, wa), lambda i: (i, 0)),
                   pl.BlockSpec((rows, wb), lambda i: (i, 0)),
                   pl.BlockSpec((rows, wb), lambda i: (i, 0)),
                   pl.BlockSpec((None, wb, rows), lambda i: (i, 0, 0)),
                   pl.BlockSpec((rows, wcw), lambda i: (i, 0)),
                   pl.BlockSpec((wo_rows, d), lambda i: (i, 0))],
        scratch_shapes=[pltpu.VMEM((rows + HALO, wcw), F32)],
        compiler_params=_params(1),
        name="inproj_mix_ac",
    )(h, w, gv, ws, bs_t, wconv, w_out)


def _split3(x):
    rnd = lambda v: v.astype(BF16).astype(F32)
    hi = rnd(x)
    mid = rnd(x - hi)
    lo = rnd(x - hi - mid)
    return hi, mid, lo


def _attn_kernel(lq_ref, gs_ref, q_ref, k_ref, vt_ref, o_ref, biasd_ref, kc_ref, qc_ref, *, lambda_init):
    hd = pl.program_id(1)
    nq, t = k_ref.shape[0], k_ref.shape[1]
    half = t // 2
    slope2 = jnp.exp2(-(jnp.full((1, t), hd + 1, jnp.int32).astype(F32))) * LOG2E

    kk = lax.broadcasted_iota(jnp.int32, (t, t), 0)
    qq = lax.broadcasted_iota(jnp.int32, (t, t), 1)
    biasd_ref[...] = jnp.where(kk <= qq, (kk - qq).astype(F32) * slope2, NEG_BIG)
    r = lax.broadcasted_iota(jnp.int32, (t, V_HEAD_DIM), 0)
    col = jnp.bitwise_and(lax.broadcasted_iota(jnp.int32, (t, V_HEAD_DIM), 1), DIFF_HEAD_DIM - 1)
    rf = r.astype(F32)
    r_hi = jnp.bitwise_and(r, -16).astype(F32)
    kc = jnp.where(col < 3, r_hi, jnp.where(col < 6, rf - r_hi, jnp.where(col < 9, 1.0, 0.0)))
    kc_ref[...] = kc.astype(BF16)
    s_col = jnp.exp2(-(jnp.full((t, V_HEAD_DIM), hd + 1, jnp.int32).astype(F32))) * LOG2E
    s3 = _split3(s_col)
    w3 = _split3(-(s_col * rf))
    qc = jnp.zeros((t, V_HEAD_DIM), F32)
    for n in range(3):
        qc = jnp.where(col == n, s3[n], qc)
        qc = jnp.where(col == n + 3, s3[n], qc)
        qc = jnp.where(col == n + 6, w3[n], qc)
    qc_ref[...] = qc.astype(BF16)

    lane = lax.broadcasted_iota(jnp.int32, (t, V_HEAD_DIM), 1)
    own = (lane < DIFF_HEAD_DIM, lane >= DIFF_HEAD_DIM)
    ones_rows = jnp.ones((16, t), BF16)
    dims = (((1,), (1,)), ((), ()))

    def q_operands(qt):
        q = q_ref[qt * t:(qt + 1) * t, :]
        zero = jnp.zeros_like(q)
        return ([jnp.where(own[mp], q, zero) for mp in range(2)],
                [jnp.where(own[mp], q, qc_ref[...]) for mp in range(2)])

    def score(q_ops, j, mp, diag):
        q_plain, q_bias = q_ops
        if diag:
            k = k_ref[j]
            first = lax.dot_general(k[0:half, :], q_plain[mp], dims, preferred_element_type=F32)
            second = lax.dot_general(k[half:, :], q_plain[mp][half:, :], dims, preferred_element_type=F32)
            masked = jnp.full((half, half), NEG_BIG, F32)
            s = jnp.concatenate([first, jnp.concatenate([masked, second], axis=1)], axis=0)
            return s + biasd_ref[...]
        k_aug = jnp.where(own[mp], k_ref[j], kc_ref[...])
        return lax.dot_general(k_aug, q_bias[mp], dims, preferred_element_type=F32)

    def update(state, s, tile_offset, j, diag):
        c = slope2 * float(tile_offset * t)
        m_new = jnp.max(s, axis=0, keepdims=True) + c
        if state is not None:
            m_old, l_old, a_old = state
            m_new = jnp.maximum(m_old, m_new)
        p = jnp.exp2(s - (m_new - c)).astype(BF16)
        vt = jnp.concatenate([vt_ref[j], ones_rows], axis=0)
        if diag:
            pv_second = _dot(vt[:, half:], p[half:, half:])
            pv = _dot(vt[:, 0:half], p[0:half, :]) + jnp.concatenate([jnp.zeros_like(pv_second), pv_second], axis=1)
        else:
            pv = _dot(vt, p)
        l_new, a_new = pv[V_HEAD_DIM:V_HEAD_DIM + 1, :], pv[0:V_HEAD_DIM, :]
        if state is not None:
            alpha = jnp.exp2(m_old - m_new)
            l_new, a_new = l_old * alpha + l_new, a_old * alpha + a_new
        return m_new, l_new, a_new

    def finish(states, qt):
        (_, l1, a1), (_, l2, a2) = states
        lq = lq_ref[...]
        s01 = jnp.sum(lq[0:1, :] * lq[1:2, :], axis=1, keepdims=True)
        s23 = jnp.sum(lq[2:3, :] * lq[3:4, :], axis=1, keepdims=True)
        lam = jnp.exp(s01) - jnp.exp(s23) + lambda_init
        o = a1 * (1.0 / l1) - lam * (a2 * (1.0 / l2))
        y = o * lax.rsqrt(jnp.mean(o * o, axis=0, keepdims=True) + EPS)
        y = y * gs_ref[...] * (1.0 - lambda_init)
        o_ref[qt * t:(qt + 1) * t, :] = y.T.astype(o_ref.dtype)

    q_ops = [q_operands(qt) for qt in range(nq)]
    work = [(qt, j, mp) for qt in range(nq) for j in range(qt + 1) for mp in range(2)]
    issue = lambda w: score(q_ops[w[0]], w[1], w[2], w[1] == w[0])
    states = {}
    pending = [issue(w) for w in work[:SCORE_LOOKAHEAD]]
    for n, (qt, j, mp) in enumerate(work):
        states[qt, mp] = update(states.get((qt, mp)), pending[n], j - qt, j, j == qt)
        if j == qt and mp == 1:
            finish((states.pop((qt, 0)), states.pop((qt, 1))), qt)
        if n + SCORE_LOOKAHEAD < len(work):
            pending.append(issue(work[n + SCORE_LOOKAHEAD]))


def _attn(q, k, vt, lq, gsub, layer, *, batch, seq, lambda_init):
    m, wb = q.shape
    t = ROW_TILE
    nq = seq // t
    heads = wb // V_HEAD_DIM
    k3 = k.reshape(m // t, t, wb)
    kern = functools.partial(_attn_kernel, lambda_init=lambda_init)
    return pl.pallas_call(
        kern,
        out_shape=jax.ShapeDtypeStruct((m, wb), BF16),
        grid=(batch, heads),
        in_specs=[
            pl.BlockSpec((None, 4, DIFF_HEAD_DIM), lambda b, h: (layer, 0, 0)),
            pl.BlockSpec((None, V_HEAD_DIM, 1), lambda b, h: (layer, 0, 0)),
            pl.BlockSpec((seq, V_HEAD_DIM), lambda b, h: (b, h)),
            pl.BlockSpec((nq, t, V_HEAD_DIM), lambda b, h: (b, 0, h)),
            pl.BlockSpec((nq, V_HEAD_DIM, t), lambda b, h: (b, h, 0)),
        ],
        out_specs=pl.BlockSpec((seq, V_HEAD_DIM), lambda b, h: (b, h)),
        scratch_shapes=[pltpu.VMEM((t, t), F32), pltpu.VMEM((t, V_HEAD_DIM), BF16),
                        pltpu.VMEM((t, V_HEAD_DIM), BF16)],
        compiler_params=_params(2),
        name="diff_attn",
    )(lq, gsub, q, k3, vt)


def _outproj_kernel(ya_ref, yb_ref, yc_ref, x_ref, w_ref, gn_ref, xo_ref, hn_ref):
    wa = ya_ref.shape[1]
    wb = yb_ref.shape[1]
    for r0 in range(0, x_ref.shape[0], NORM_ROWS):
        rs = slice(r0, r0 + NORM_ROWS)
        acc = x_ref[rs, :] + _dot(ya_ref[rs, :], w_ref[0:wa, :])
        acc = acc + _dot(yb_ref[rs, :], w_ref[wa:wa + wb, :])
        acc = acc + _dot(yc_ref[rs, :], w_ref[wa + wb:, :])
        xo_ref[rs, :] = acc
        hn_ref[rs, :] = _rms(acc, gn_ref[...]).astype(hn_ref.dtype)


def _outproj(ya, yb, yc, x, w, gains, layer):
    m, d = x.shape
    rows = ROW_TILE
    row_block = lambda width: pl.BlockSpec((rows, width), lambda i: (i, 0))
    return pl.pallas_call(
        _outproj_kernel,
        out_shape=[jax.ShapeDtypeStruct((m, d), F32), jax.ShapeDtypeStruct((m, d), BF16)],
        grid=(m // rows,),
        in_specs=[row_block(ya.shape[1]), row_block(yb.shape[1]), row_block(yc.shape[1]), row_block(d),
                  _resident(w.shape, lambda i: (0, 0)),
                  pl.BlockSpec((None, 1, d), lambda i: (layer, 0, 0))],
        out_specs=[row_block(d), row_block(d)],
        compiler_params=_params(1),
        name="outproj",
    )(ya, yb, yc, x, w, gains)


def kernel(x, g_ffn1, w_ffn1_gate, w_ffn1_up, w_ffn1_down, g_mix, w_in, g_sga_v, w_sga_s, b_sga_s,
           lambda_qk, g_diff_sub, w_conv, w_out, g_ffn2, w_ffn2_gate, w_ffn2_up, w_ffn2_down, g_final):
    batch, seq, d = x.shape
    depth = w_in.shape[0]
    wa = g_sga_v.shape[-1]
    wb = w_out.shape[1] - 2 * wa
    assert seq % ROW_TILE == 0 and wb % V_HEAD_DIM == 0

    cast = lambda w: w.astype(BF16)
    ffn1_f32 = (w_ffn1_gate, w_ffn1_up, w_ffn1_down)
    ffn2_f32 = (w_ffn2_gate, w_ffn2_up, w_ffn2_down)
    ffn_casts = lambda ws, layer: [(ws[0], layer, False), (ws[1], layer, False), (ws[2], layer, True)]
    w_ffn = tuple(cast(w[0]) for w in ffn1_f32)
    w_in_l = cast(w_in[0])
    row = lambda g: g.reshape(g.shape[0], 1, g.shape[-1])
    g1, gm, g2, gv = row(g_ffn1), row(g_mix), row(g_ffn2), row(g_sga_v)
    gf = g_final.reshape(1, 1, d)
    bs_t = jnp.swapaxes(b_sga_s, 1, 2)
    gsub = g_diff_sub.reshape(depth, V_HEAD_DIM, 1)

    xf = x.reshape(batch * seq, d)
    h = _norm(xf, g1, 0)
    for l in range(depth):
        xf, h, *w_ffn = _ffn(h, xf, w_ffn, gm, l, casts=ffn_casts(ffn2_f32, l))
        ya, q, k, vt, yc, wo = _inproj(h, w_in_l, gv, w_sga_s, bs_t, w_conv, w_out, l, seq=seq, wa=wa, wb=wb)
        lambda_init = 0.8 - 0.6 * math.exp(-0.3 * l)
        yb = _attn(q, k, vt, lambda_qk, gsub, l, batch=batch, seq=seq, lambda_init=lambda_init)
        xf, h = _outproj(ya, yb, yc, xf, wo, g2, l)
        if l + 1 < depth:
            xf, h, *w_next = _ffn(h, xf, w_ffn, g1, l + 1,
                                  casts=ffn_casts(ffn1_f32, l + 1) + [(w_in, l + 1, False)])
            w_ffn, w_in_l = w_next[0:3], w_next[3]
        else:
            out = _ffn(h, xf, w_ffn, gf, 0)
    return out.reshape(batch, seq, d)
```
